```python
import jax
import jax.numpy as jnp
from jax import lax
import numpy as np

D_MODEL = 1024
BATCH = 16
SEQ = 4096
DEPTH = 2

GRID_W = 64
CTX_LEN = 256
HEAD_DIM = 64
ROPE_THETA = 10000.0
A_HEADS = 8
A_KV_HEADS = 2
Q_BLOCK = 128
NA_HEADS = 8
NA_WIN_R = 8
NA_WIN_C = 16
ML_HEADS = 4
ML_HEAD_DIM = 128
ML_CHUNK = 64
N_BRANCH = 3
BRANCH_W = 512
N_GROUPS = 4
EXPERTS_PER_GROUP = 8
N_EXPERTS = N_GROUPS * EXPERTS_PER_GROUP
TOP_K = 2
D_EXPERT = 256
EPS = 1e-6
NEG_INF = -1e30
IN_SPLITS = (
    A_HEADS * HEAD_DIM, A_KV_HEADS * HEAD_DIM, A_KV_HEADS * HEAD_DIM,
    NA_HEADS * HEAD_DIM, NA_HEADS * HEAD_DIM, NA_HEADS * HEAD_DIM,
    ML_HEADS * ML_HEAD_DIM, ML_HEADS * ML_HEAD_DIM, ML_HEADS * ML_HEAD_DIM, ML_HEADS * ML_HEAD_DIM,
    2 * 2 * ML_HEADS,
    N_BRANCH * D_MODEL,
)
IN_WIDTH = sum(IN_SPLITS)

kernel_name = 'hybrid_gqa_natten_mlstm_hmoe_dit'


def rms_norm(x, gain):
    x32 = x.astype(jnp.float32)
    y = x32 * lax.rsqrt(jnp.mean(x32 * x32, axis=-1, keepdims=True) + EPS)
    return (y * gain.astype(jnp.float32)).astype(x.dtype)


def modulate(h, shift, scale):
    return h * (1 + scale) + shift


def split_cols(p):
    out, start = [], 0
    for w in IN_SPLITS:
        out.append(p[..., start:start + w])
        start += w
    return out


def to_heads(a, n_heads, head_dim):
    return a.reshape(a.shape[:2] + (n_heads, head_dim))


def axial_rope(n_tok, dtype):
    t = jnp.arange(n_tok)
    row = (t // GRID_W).astype(jnp.float32)
    col = (t % GRID_W).astype(jnp.float32)
    n_freq = HEAD_DIM // 4
    inv = ROPE_THETA ** (-jnp.arange(n_freq, dtype=jnp.float32) / n_freq)
    ang = jnp.concatenate([row[:, None] * inv, col[:, None] * inv], axis=-1)
    return jnp.cos(ang).astype(dtype), jnp.sin(ang).astype(dtype)


def apply_rope(x, cos, sin):
    half = x.shape[-1] // 2
    x1, x2 = x[..., :half], x[..., half:]
    cs, sn = cos[None, :, None, :], sin[None, :, None, :]
    return jnp.concatenate([x1 * cs - x2 * sn, x2 * cs + x1 * sn], axis=-1)


def gqa_attend(q, k, v):
    s = jnp.einsum('bqkgd,bskd->bkgqs', q, k).astype(jnp.float32) * HEAD_DIM ** -0.5
    p = jax.nn.softmax(s, axis=-1).astype(v.dtype)
    return jnp.einsum('bkgqs,bskd->bqkgd', p, v)


def gqa_branch(q, k, v, q_c, k_c, v_c, need_ctx):
    bsz, n_tok = q.shape[:2]
    grp = A_HEADS // A_KV_HEADS
    kk = jnp.concatenate([k, k_c], axis=1)
    vv = jnp.concatenate([v, v_c], axis=1)
    nb = n_tok // Q_BLOCK
    qb = q.reshape(bsz, nb, Q_BLOCK, A_KV_HEADS, grp, HEAD_DIM).swapaxes(0, 1)
    o = lax.map(lambda qq: gqa_attend(qq, kk, vv), qb)
    y = o.swapaxes(0, 1).reshape(bsz, n_tok, A_HEADS * HEAD_DIM)
    yc = None
    if need_ctx:
        n_ctx = q_c.shape[1]
        yc = gqa_attend(q_c.reshape(bsz, n_ctx, A_KV_HEADS, grp, HEAD_DIM), k_c, v_c)
        yc = yc.reshape(bsz, n_ctx, A_HEADS * HEAD_DIM)
    return y, yc


def na_branch(q, k, v, q_c, k_c, v_c, rpb, rows, need_ctx):
    bsz = q.shape[0]
    win_r = min(NA_WIN_R, rows)
    qg = q.reshape(bsz, rows, GRID_W, NA_HEADS, HEAD_DIM).swapaxes(0, 1)
    kg = k.reshape(bsz, rows, GRID_W, NA_HEADS, HEAD_DIM)
    vg = v.reshape(bsz, rows, GRID_W, NA_HEADS, HEAD_DIM)
    col = jnp.arange(GRID_W)
    col_start = jnp.clip(col - NA_WIN_C // 2, 0, GRID_W - NA_WIN_C)
    col_mask = (col[None, :] >= col_start[:, None]) & (col[None, :] < col_start[:, None] + NA_WIN_C)
    dc_idx = jnp.clip(col[None, :] - col[:, None] + NA_WIN_C - 1, 0, 2 * NA_WIN_C - 2)
    scale = HEAD_DIM ** -0.5

    def row_block(args):
        r, qb = args
        rs = jnp.clip(r - win_r // 2, 0, rows - win_r)
        kb = lax.dynamic_slice_in_dim(kg, rs, win_r, axis=1)
        vb = lax.dynamic_slice_in_dim(vg, rs, win_r, axis=1)
        dr_idx = rs + jnp.arange(win_r) - r + NA_WIN_R - 1
        bias = rpb[:, dr_idx[None, :, None], dc_idx[:, None, :]].astype(jnp.float32)
        s_loc = jnp.einsum('bqhd,bjkhd->bhqjk', qb, kb).astype(jnp.float32) * scale + bias
        s_loc = jnp.where(col_mask[:, None, :], s_loc, NEG_INF)
        s_loc = s_loc.reshape(bsz, NA_HEADS, GRID_W, win_r * GRID_W)
        s_ctx = jnp.einsum('bqhd,bshd->bhqs', qb, k_c).astype(jnp.float32) * scale
        p = jax.nn.softmax(jnp.concatenate([s_loc, s_ctx], axis=-1), axis=-1).astype(v.dtype)
        p_loc = p[..., :win_r * GRID_W].reshape(bsz, NA_HEADS, GRID_W, win_r, GRID_W)
        p_ctx = p[..., win_r * GRID_W:]
        return (jnp.einsum('bhqjk,bjkhd->bqhd', p_loc, vb)
                + jnp.einsum('bhqs,bshd->bqhd', p_ctx, v_c))

    o = lax.map(row_block, (jnp.arange(rows), qg))
    y = o.swapaxes(0, 1).reshape(bsz, rows * GRID_W, NA_HEADS * HEAD_DIM)
    yc = None
    if need_ctx:
        n_ctx = q_c.shape[1]
        yc = gqa_attend(q_c[:, :, :, None, :], k_c, v_c).reshape(bsz, n_ctx, NA_HEADS * HEAD_DIM)
    return y, yc


def mlstm_chunk_step(carry, xs):
    c_mat, n_vec, m = carry
    q, k, v, ig, lf = xs
    lc = q.shape[2]
    lower = jnp.tril(jnp.ones((lc, lc), dtype=bool))
    b = jnp.cumsum(lf, axis=-1)
    a = b + m[..., None]
    dmat = jnp.where(lower, b[..., :, None] - b[..., None, :] + ig[..., None, :], NEG_INF)
    m_t = jnp.maximum(a, dmat.max(axis=-1))
    w_inter = jnp.exp(a - m_t)
    s = jnp.einsum('bhtd,bhsd->bhts', q, k) * jnp.exp(dmat - m_t[..., None])
    num = w_inter[..., None] * jnp.einsum('bhtd,bhde->bhte', q, c_mat) + jnp.einsum('bhts,bhse->bhte', s, v)
    den = w_inter * jnp.einsum('bhtd,bhd->bht', q, n_vec) + s.sum(axis=-1)
    h = num / jnp.maximum(jnp.abs(den), jnp.exp(-m_t))[..., None]
    b_end = b[..., -1]
    log_w = b_end[..., None] - b + ig
    m_new = jnp.maximum(b_end + m, log_w.max(axis=-1))
    g_inter = jnp.exp(b_end + m - m_new)
    g_s = jnp.exp(log_w - m_new[..., None])
    kg = k * g_s[..., None]
    c_new = g_inter[..., None, None] * c_mat + jnp.einsum('bhsd,bhse->bhde', kg, v)
    n_new = g_inter[..., None] * n_vec + kg.sum(axis=2)
    return (c_new, n_new, m_new), h


def mlstm_scan(q, k, v, ig, lf, state):
    bsz, n_tok, nh, hd = q.shape
    nc = n_tok // ML_CHUNK
    chunks = lambda a: a.reshape(bsz, nc, ML_CHUNK, nh, hd).transpose(1, 0, 3, 2, 4)
    gchunks = lambda a: a.reshape(bsz, nc, ML_CHUNK, nh).transpose(1, 0, 3, 2)
    state, h = lax.scan(mlstm_chunk_step, state, (chunks(q), chunks(k), chunks(v), gchunks(ig), gchunks(lf)))
    return h.transpose(1, 0, 3, 2, 4).reshape(bsz, n_tok, nh, hd), state


def maybe_flip(a, direction):
    return jnp.flip(a, axis=1) if direction == 1 else a


def mlstm_branch(q, k, v, o, pre, q_c, k_c, v_c, o_c, pre_c, g_ml, need_ctx):
    dt = q.dtype
    bsz = q.shape[0]
    f32 = jnp.float32
    qs, vs, ks = q.astype(f32), v.astype(f32), k.astype(f32) * ML_HEAD_DIM ** -0.5
    qcs, vcs, kcs = q_c.astype(f32), v_c.astype(f32), k_c.astype(f32) * ML_HEAD_DIM ** -0.5
    ig, lf = pre[:, :, :, 0, :], jax.nn.log_sigmoid(pre[:, :, :, 1, :])
    ig_c, lf_c = pre_c[:, :, :, 0, :], jax.nn.log_sigmoid(pre_c[:, :, :, 1, :])
    state0 = (jnp.zeros((bsz, ML_HEADS, ML_HEAD_DIM, ML_HEAD_DIM), f32),
              jnp.zeros((bsz, ML_HEADS, ML_HEAD_DIM), f32),
              jnp.zeros((bsz, ML_HEADS), f32))
    h_lat = jnp.zeros(qs.shape, f32)
    h_ctx = jnp.zeros(qcs.shape, f32)
    for d in range(2):
        hc_d, st = mlstm_scan(maybe_flip(qcs, d), maybe_flip(kcs, d), maybe_flip(vcs, d),
                              maybe_flip(ig_c[:, :, d], d), maybe_flip(lf_c[:, :, d], d), state0)
        hl_d, _ = mlstm_scan(maybe_flip(qs, d), maybe_flip(ks, d), maybe_flip(vs, d),
                             maybe_flip(ig[:, :, d], d), maybe_flip(lf[:, :, d], d), st)
        h_lat = h_lat + maybe_flip(hl_d, d)
        h_ctx = h_ctx + maybe_flip(hc_d, d)
    gain = g_ml.reshape(ML_HEADS, ML_HEAD_DIM)

    def finish(hsum, og):
        hn = rms_norm(hsum, gain).astype(dt)
        out = hn * jax.nn.sigmoid(og).reshape(hn.shape)
        return out.reshape(hn.shape[0], hn.shape[1], ML_HEADS * ML_HEAD_DIM)

    y = finish(h_lat, o)
    yc = finish(h_ctx, o_c) if need_ctx else None
    return y, yc


def token_mixers(h, hc, w_in, b_merge, g_qk, rpb, b_mlstm, g_ml, w_branch, w_out, rows, cos, sin, need_ctx):
    bsz, n_tok, _ = h.shape
    n_ctx = hc.shape[1]
    pl = split_cols(h @ w_in)
    pc = split_cols(hc @ w_in)
    qa = apply_rope(rms_norm(to_heads(pl[0], A_HEADS, HEAD_DIM), g_qk[0]), cos, sin)
    ka = apply_rope(rms_norm(to_heads(pl[1], A_KV_HEADS, HEAD_DIM), g_qk[1]), cos, sin)
    va = to_heads(pl[2], A_KV_HEADS, HEAD_DIM)
    qa_c = rms_norm(to_heads(pc[0], A_HEADS, HEAD_DIM), g_qk[0])
    ka_c = rms_norm(to_heads(pc[1], A_KV_HEADS, HEAD_DIM), g_qk[1])
    va_c = to_heads(pc[2], A_KV_HEADS, HEAD_DIM)
    ya, ya_c = gqa_branch(qa, ka, va, qa_c, ka_c, va_c, need_ctx)
    qb = rms_norm(to_heads(pl[3], NA_HEADS, HEAD_DIM), g_qk[2])
    kb = rms_norm(to_heads(pl[4], NA_HEADS, HEAD_DIM), g_qk[3])
    vb = to_heads(pl[5], NA_HEADS, HEAD_DIM)
    qb_c = rms_norm(to_heads(pc[3], NA_HEADS, HEAD_DIM), g_qk[2])
    kb_c = rms_norm(to_heads(pc[4], NA_HEADS, HEAD_DIM), g_qk[3])
    vb_c = to_heads(pc[5], NA_HEADS, HEAD_DIM)
    yb, yb_c = na_branch(qb, kb, vb, qb_c, kb_c, vb_c, rpb, rows, need_ctx)
    pre = (pl[10].reshape(bsz, n_tok, 2, 2, ML_HEADS) + b_mlstm).astype(jnp.float32)
    pre_c = (pc[10].reshape(bsz, n_ctx, 2, 2, ML_HEADS) + b_mlstm).astype(jnp.float32)
    ym, ym_c = mlstm_branch(to_heads(pl[6], ML_HEADS, ML_HEAD_DIM), to_heads(pl[7], ML_HEADS, ML_HEAD_DIM),
                            to_heads(pl[8], ML_HEADS, ML_HEAD_DIM), pl[9], pre,
                            to_heads(pc[6], ML_HEADS, ML_HEAD_DIM), to_heads(pc[7], ML_HEADS, ML_HEAD_DIM),
                            to_heads(pc[8], ML_HEADS, ML_HEAD_DIM), pc[9], pre_c, g_ml, need_ctx)

    def merge(outs, gate_cols, n):
        gates = jax.nn.sigmoid(gate_cols.reshape(bsz, n, N_BRANCH, D_MODEL) + b_merge)
        merged = gates[:, :, 0] * (outs[0] @ w_branch[0])
        for br in range(1, N_BRANCH):
            merged = merged + gates[:, :, br] * (outs[br] @ w_branch[br])
        return merged @ w_out

    y = merge((ya, yb, ym), pl[11], n_tok)
    yc = merge((ya_c, yb_c, ym_c), pc[11], n_ctx) if need_ctx else None
    return y, yc


def hier_moe(h, w_group, b_group, w_router, b_router, w_gate_up, w_down):
    shp = h.shape
    t = h.reshape(-1, D_MODEL)
    lg = (t @ w_group + b_group).astype(jnp.float32)
    pg = jax.nn.softmax(lg, axis=-1)
    gsel = jnp.argmax(lg, axis=-1)
    le = (t @ w_router + b_router).astype(jnp.float32).reshape(-1, N_GROUPS, EXPERTS_PER_GROUP)
    le_sel = jnp.take_along_axis(le, gsel[:, None, None], axis=1)[:, 0]
    pe = jax.nn.softmax(le_sel, axis=-1)
    top_p, top_i = lax.top_k(pe, TOP_K)
    pg_sel = jnp.take_along_axis(pg, gsel[:, None], axis=1)
    wts = pg_sel * top_p / jnp.sum(top_p, axis=-1, keepdims=True)
    eid = gsel[:, None] * EXPERTS_PER_GROUP + top_i
    dense_w = jnp.sum(jax.nn.one_hot(eid, N_EXPERTS, dtype=jnp.float32) * wts[..., None], axis=1).astype(t.dtype)
    out = jnp.zeros_like(t)
    for e in range(N_EXPERTS):
        gu = t @ w_gate_up[e]
        y = (jax.nn.silu(gu[:, :D_EXPERT]) * gu[:, D_EXPERT:]) @ w_down[e]
        out = out + dense_w[:, e:e + 1] * y
    return out.reshape(shp)


def setup_inputs(seed: int = 0) -> dict:
    key = jax.random.key(seed)
    ks = jax.random.split(key, 24)
    f32 = jnp.float32
    nrm = lambda k, shape, s: jax.random.normal(k, shape, f32) * s
    b_ml_i = nrm(ks[11], (DEPTH, 2, ML_HEADS), 0.1)
    b_ml_f = 3.0 + 3.0 * jax.random.uniform(ks[12], (DEPTH, 2, ML_HEADS), f32)
    return {
        'x': nrm(ks[0], (BATCH, SEQ, D_MODEL), 1.0),
        'c': nrm(ks[1], (BATCH, D_MODEL), 1.0),
        'ctx': nrm(ks[2], (BATCH, CTX_LEN, D_MODEL), 1.0),
        'c_ctx': nrm(ks[3], (D_MODEL,), 1.0),
        'w_mod': nrm(ks[4], (DEPTH, D_MODEL, 6 * D_MODEL), 0.5 * D_MODEL ** -0.5),
        'b_mod': nrm(ks[5], (DEPTH, 6 * D_MODEL), 0.02),
        'g_norm': 1.0 + nrm(ks[6], (DEPTH, 2, D_MODEL), 0.02),
        'w_in': nrm(ks[7], (DEPTH, D_MODEL, IN_WIDTH), D_MODEL ** -0.5),
        'b_merge': nrm(ks[8], (DEPTH, N_BRANCH, D_MODEL), 0.02),
        'g_qk': 1.0 + nrm(ks[9], (DEPTH, 4, HEAD_DIM), 0.02),
        'rpb': nrm(ks[10], (DEPTH, NA_HEADS, 2 * NA_WIN_R - 1, 2 * NA_WIN_C - 1), 0.1),
        'b_mlstm': jnp.stack([b_ml_i, b_ml_f], axis=2),
        'g_ml': 1.0 + nrm(ks[13], (DEPTH, ML_HEADS * ML_HEAD_DIM), 0.02),
        'w_branch': nrm(ks[14], (DEPTH, N_BRANCH, BRANCH_W, D_MODEL), BRANCH_W ** -0.5),
        'w_out': nrm(ks[15], (DEPTH, D_MODEL, D_MODEL), D_MODEL ** -0.5),
        'w_group': nrm(ks[16], (DEPTH, D_MODEL, N_GROUPS), D_MODEL ** -0.5),
        'b_group': nrm(ks[17], (DEPTH, N_GROUPS), 0.01),
        'w_router': nrm(ks[18], (DEPTH, D_MODEL, N_EXPERTS), D_MODEL ** -0.5),
        'b_router': nrm(ks[19], (DEPTH, N_EXPERTS), 0.01),
        'w_gate_up': nrm(ks[20], (DEPTH, N_EXPERTS, D_MODEL, 2 * D_EXPERT), D_MODEL ** -0.5),
        'w_down': nrm(ks[21], (DEPTH, N_EXPERTS, D_EXPERT, D_MODEL), D_EXPERT ** -0.5),
    }


def reference(x, c, ctx, c_ctx, w_mod, b_mod, g_norm, w_in, b_merge, g_qk, rpb, b_mlstm, g_ml,
              w_branch, w_out, w_group, b_group, w_router, b_router, w_gate_up, w_down):
    n_tok = x.shape[1]
    rows = n_tok // GRID_W
    cos, sin = axial_rope(n_tok, x.dtype)
    xc = ctx
    for l in range(DEPTH):
        need_ctx = l < DEPTH - 1
        mod = jax.nn.silu(c) @ w_mod[l] + b_mod[l]
        mod_c = jax.nn.silu(c_ctx) @ w_mod[l] + b_mod[l]
        sh1, sc1, gt1, sh2, sc2, gt2 = jnp.split(mod[:, None, :], 6, axis=-1)
        sh1c, sc1c, gt1c, sh2c, sc2c, gt2c = jnp.split(mod_c, 6, axis=-1)
        h = modulate(rms_norm(x, g_norm[l, 0]), sh1, sc1)
        hc = modulate(rms_norm(xc, g_norm[l, 0]), sh1c, sc1c)
        y, yc = token_mixers(h, hc, w_in[l], b_merge[l], g_qk[l], rpb[l], b_mlstm[l], g_ml[l],
                             w_branch[l], w_out[l], rows, cos, sin, need_ctx)
        x = x + gt1 * y
        h = modulate(rms_norm(x, g_norm[l, 1]), sh2, sc2)
        x = x + gt2 * hier_moe(h, w_group[l], b_group[l], w_router[l], b_router[l], w_gate_up[l], w_down[l])
        if need_ctx:
            xc = xc + gt1c * yc
            hc = modulate(rms_norm(xc, g_norm[l, 1]), sh2c, sc2c)
            xc = xc + gt2c * hier_moe(hc, w_group[l], b_group[l], w_router[l], b_router[l], w_gate_up[l], w_down[l])
    return x
```

```python
import functools

import numpy as np
import jax
import jax.numpy as jnp
from jax import lax
from jax.experimental import pallas as pl
from jax.experimental.pallas import tpu as pltpu

D_MODEL = 1024
GRID_W = 64
HEAD_DIM = 64
ROPE_THETA = 10000.0
A_HEADS = 8
A_KV_HEADS = 2
NA_HEADS = 8
NA_WIN_R = 8
NA_WIN_C = 16
ML_HEADS = 4
ML_HEAD_DIM = 128
N_BRANCH = 3
BRANCH_W = 512
N_GROUPS = 4
EXPERTS_PER_GROUP = 8
N_EXPERTS = N_GROUPS * EXPERTS_PER_GROUP
D_EXPERT = 256
EPS = 1e-6
NEG_INF = -1e30

TILE = 256
LANES = 128
NA_WIN_TILES = 3
GATE_W = 128
V7X_VMEM_BYTES = 64 * 1024 * 1024

F32 = jnp.float32
BF16 = jnp.bfloat16

_O_QA, _O_KA, _O_VA = 0, 512, 640
_O_QB, _O_KB, _O_VB = 768, 1280, 1792
_O_CQ, _O_CK, _O_CV, _O_CO = 2304, 2816, 3328, 3840
_O_CG = 4352
_O_MG = _O_CG + GATE_W
IN_WIDTH_P = _O_MG + N_BRANCH * D_MODEL
_A_HEAD_ORDER = (0, 4, 1, 5, 2, 6, 3, 7)


def _cparams(sem, vmem_mb):
    return pltpu.CompilerParams(dimension_semantics=sem, vmem_limit_bytes=vmem_mb * 1024 * 1024)


def _dot(a, b):
    return jnp.dot(a, b, preferred_element_type=F32)


def _dot_nt(a, b):
    return lax.dot_general(a, b, (((1,), (1,)), ((), ())), preferred_element_type=F32)


def _dot_tn(a, b):
    return lax.dot_general(a, b, (((0,), (0,)), ((), ())), preferred_element_type=F32)


def _split(a):
    hi = a.astype(BF16)
    lo = (a - hi.astype(F32)).astype(BF16)
    return hi, lo


def _sigmoid(x):
    return 1.0 / (1.0 + jnp.exp(-x))


def _rms_mod(x, g, shift, scale):
    ms = jnp.mean(x * x, axis=-1, keepdims=True)
    y = x * lax.rsqrt(ms + EPS) * g
    return y * (1.0 + scale) + shift


def _mod_kernel(c_ref, w_ref, b_ref, o_ref):
    c = c_ref[...]
    a_hi, a_lo = _split(c * _sigmoid(c))
    w_hi, w_lo = _split(w_ref[...])
    o_ref[...] = _dot(a_hi, w_hi) + _dot(a_hi, w_lo) + _dot(a_lo, w_hi) + b_ref[...]


def _modulation(cc, w_mod, b_mod):
    depth, d, n = w_mod.shape
    r = cc.shape[0]
    bn = 1024
    return pl.pallas_call(
        _mod_kernel,
        grid=(depth, n // bn),
        in_specs=[
            pl.BlockSpec((r, d), lambda l, j: (0, 0)),
            pl.BlockSpec((None, d, bn), lambda l, j: (l, 0, j)),
            pl.BlockSpec((None, 1, bn), lambda l, j: (l, 0, j)),
        ],
        out_specs=pl.BlockSpec((None, r, bn), lambda l, j: (l, 0, j)),
        out_shape=jax.ShapeDtypeStruct((depth, r, n), F32),
        compiler_params=_cparams(("arbitrary", "arbitrary"), 32),
    )(cc, w_mod, b_mod.reshape(depth, 1, n))


def _group_norm64(acc, gmat, gain):
    hi, lo = _split(acc * acc)
    ss = _dot(hi, gmat) + _dot(lo, gmat)
    return acc * lax.rsqrt(ss * (1.0 / HEAD_DIM) + EPS) * gain


def _rope(x, cos, sin_signed):
    w = x.shape[1]
    lane = lax.broadcasted_iota(jnp.int32, x.shape, 1)
    nxt = pltpu.roll(x, w - HEAD_DIM // 2, 1)
    prv = pltpu.roll(x, HEAD_DIM // 2, 1)
    rot = jnp.where((lane % HEAD_DIM) < HEAD_DIM // 2, nxt, prv)
    return x * cos + rot * sin_signed


def _in_kernel(x_ref, mod_ref, g_ref, w_ref, gmat_ref, gqk_ref, cos_ref, sin_ref, bcg_ref, bmg_ref,
               qa_ref, kva_ref, qb_ref, kb_ref, vb_ref, cq_ref, ck_ref, cv_ref, co_ref, cg_ref, mg_ref):
    d = D_MODEL
    mod = mod_ref[...]
    h = _rms_mod(x_ref[...], g_ref[...], mod[:, 0:d], mod[:, d:2 * d]).astype(BF16)

    def proj(a, width):
        return _dot(h, w_ref[:, a:a + width])

    gmat = gmat_ref[...]
    cos = cos_ref[...]
    sin = sin_ref[...]
    att_scale = HEAD_DIM ** -0.5
    for s in range(2):
        acc = _group_norm64(proj(_O_QA + 256 * s, 256), gmat, gqk_ref[0:1, :])
        qa_ref[:, 256 * s:256 * (s + 1)] = (_rope(acc, cos, sin) * att_scale).astype(BF16)
    acc = _group_norm64(proj(_O_KA, 128), gmat[:128, :128], gqk_ref[1:2, :128])
    kva_ref[:, 0:128] = _rope(acc, cos[:, :128], sin[:, :128]).astype(BF16)
    kva_ref[:, 128:256] = proj(_O_VA, 128).astype(BF16)
    for s in range(2):
        acc = _group_norm64(proj(_O_QB + 256 * s, 256), gmat, gqk_ref[2:3, :])
        qb_ref[:, 256 * s:256 * (s + 1)] = (acc * att_scale).astype(BF16)
        acc = _group_norm64(proj(_O_KB + 256 * s, 256), gmat, gqk_ref[3:4, :])
        kb_ref[:, 256 * s:256 * (s + 1)] = acc.astype(BF16)
    vb_ref[...] = proj(_O_VB, 512).astype(BF16)
    cq_ref[...] = proj(_O_CQ, 512).astype(BF16)
    ck_ref[...] = (proj(_O_CK, 512) * (ML_HEAD_DIM ** -0.5)).astype(BF16)
    cv_ref[...] = proj(_O_CV, 512).astype(BF16)
    co_ref[...] = proj(_O_CO, 512).astype(BF16)
    cg_ref[...] = proj(_O_CG, GATE_W) + bcg_ref[...]
    for s in range(N_BRANCH * D_MODEL // 512):
        a = 512 * s
        mg_ref[:, a:a + 512] = _sigmoid(proj(_O_MG + a, 512) + bmg_ref[:, a:a + 512]).astype(BF16)


def _tile_mod_index(nt, nb):
    return lambda i: (jnp.where(i % nt == 0, nb, i // nt), 0, 0)


def _in_projection(xs, mods, g1, w_in_p, gmat, gqk, cos_t, sin_t, bcg, bmg, nb, nt):
    t = xs.shape[0]
    d = D_MODEL
    row = lambda w: pl.BlockSpec((TILE, w), lambda i: (i, 0))
    full = lambda a: pl.BlockSpec(a.shape, lambda i: (0,) * a.ndim)
    widths = (512, 256, 512, 512, 512, 512, 512, 512, 512)
    out_shape = [jax.ShapeDtypeStruct((t, w), BF16) for w in widths]
    out_shape += [jax.ShapeDtypeStruct((t, GATE_W), F32), jax.ShapeDtypeStruct((t, N_BRANCH * d), BF16)]
    out_specs = [row(w) for w in widths] + [row(GATE_W), row(N_BRANCH * d)]
    return pl.pallas_call(
        _in_kernel,
        grid=(t // TILE,),
        in_specs=[
            row(d),
            pl.BlockSpec((None, 1, 6 * d), _tile_mod_index(nt, nb)),
            full(g1),
            pl.BlockSpec(w_in_p.shape, lambda i: (0, 0), pipeline_mode=pl.Buffered(1)),
            full(gmat),
            full(gqk),
            pl.BlockSpec((TILE, 256), lambda i: (i % nt, 0)),
            pl.BlockSpec((TILE, 256), lambda i: (i % nt, 0)),
            full(bcg),
            full(bmg),
        ],
        out_specs=out_specs,
        out_shape=out_shape,
        compiler_params=_cparams(("arbitrary",), 52),
    )(xs, mods, g1, w_in_p, gmat, gqk, cos_t, sin_t, bcg, bmg)


def _attn_a_kernel(q_ref, kv_ref, o_ref):
    qt = pl.program_id(1)
    left = lax.broadcasted_iota(jnp.int32, (TILE, LANES), 1) < HEAD_DIM
    qblk = q_ref[...]

    def run(n_keys):
        k = kv_ref[0:n_keys, 0:LANES]
        v = kv_ref[0:n_keys, LANES:2 * LANES]
        outs = []
        for half in range(2):
            qh = jnp.where(left if half == 0 else jnp.logical_not(left), qblk, jnp.zeros_like(qblk))
            s = _dot_nt(qh, k)
            p = jnp.exp(s - jnp.max(s, axis=-1, keepdims=True))
            o = _dot(p.astype(BF16), v)
            outs.append(o / jnp.sum(p, axis=-1, keepdims=True))
        o_ref[...] = jnp.where(left, outs[0], outs[1]).astype(BF16)

    @pl.when(qt == 0)
    def _():
        run(TILE)

    @pl.when(qt > 0)
    def _():
        run(kv_ref.shape[0])


def _attn_a(qa, kva, nb, nt):
    t = qa.shape[0]
    s = nt * TILE
    return pl.pallas_call(
        _attn_a_kernel,
        grid=(nb, nt, A_HEADS // 2),
        in_specs=[
            pl.BlockSpec((TILE, LANES), lambda b, i, p: (b * nt + i, p)),
            pl.BlockSpec((None, s, 2 * LANES), lambda b, i, p: (b, 0, 0)),
        ],
        out_specs=pl.BlockSpec((TILE, LANES), lambda b, i, p: (b * nt + i, p)),
        out_shape=jax.ShapeDtypeStruct((t, A_HEADS * HEAD_DIM), BF16),
        compiler_params=_cparams(("arbitrary",) * 3, 48),
    )(qa, kva.reshape(nb, s, 2 * LANES))


def _attn_b_kernel(q_ref, k0_ref, k1_ref, k2_ref, kc_ref, v0_ref, v1_ref, v2_ref, vc_ref, tab_ref, o_ref):
    left = lax.broadcasted_iota(jnp.int32, (TILE, LANES), 1) < HEAD_DIM
    k_refs = (k0_ref, k1_ref, k2_ref, kc_ref)
    v_refs = (v0_ref, v1_ref, v2_ref, vc_ref)
    for p in range(NA_HEADS // 2):
        sl = slice(p * LANES, (p + 1) * LANES)
        qblk = q_ref[:, sl]
        ks = [r[:, sl] for r in k_refs]
        vs = [r[:, sl] for r in v_refs]
        outs = []
        for half in range(2):
            hd = 2 * p + half
            qh = jnp.where(left if half == 0 else jnp.logical_not(left), qblk, jnp.zeros_like(qblk))
            ss = [_dot_nt(qh, kk) for kk in ks]
            for j in range(NA_WIN_TILES):
                ss[j] = ss[j] + tab_ref[hd, :, j * TILE:(j + 1) * TILE]
            m = functools.reduce(jnp.maximum, [jnp.max(s, axis=-1, keepdims=True) for s in ss])
            ps = [jnp.exp(s - m) for s in ss]
            l = functools.reduce(jnp.add, [jnp.sum(pp, axis=-1, keepdims=True) for pp in ps])
            o = functools.reduce(jnp.add, [_dot(pp.astype(BF16), vv) for pp, vv in zip(ps, vs)])
            outs.append(o / l)
        o_ref[:, sl] = jnp.where(left, outs[0], outs[1]).astype(BF16)


def _attn_b(qb, kb, vb, table, nb, nt):
    t = qb.shape[0]
    w = NA_HEADS * HEAD_DIM
    n_lat = nt - 1

    def win(j):
        return lambda i, b: (b * nt + 1 + jnp.clip(i - 2, 0, n_lat - NA_WIN_TILES) + j, 0)

    ctx = lambda i, b: (b * nt, 0)
    variant = lambda i, b: (jnp.where(i == 0, 3, jnp.where(i == 1, 0, jnp.where(i == n_lat, 2, 1))), 0, 0, 0)
    blk = lambda f: pl.BlockSpec((TILE, w), f)
    return pl.pallas_call(
        _attn_b_kernel,
        grid=(nt, nb),
        in_specs=[blk(lambda i, b: (b * nt + i, 0)),
                  blk(win(0)), blk(win(1)), blk(win(2)), blk(ctx),
                  blk(win(0)), blk(win(1)), blk(win(2)), blk(ctx),
                  pl.BlockSpec((None, NA_HEADS, TILE, NA_WIN_TILES * TILE), variant)],
        out_specs=blk(lambda i, b: (b * nt + i, 0)),
        out_shape=jax.ShapeDtypeStruct((t, w), BF16),
        compiler_params=_cparams(("arbitrary", "arbitrary"), 48),
    )(qb, kb, kb, kb, kb, vb, vb, vb, vb, table)


def _na_table(rpb, rows):
    rq = TILE // GRID_W
    n_lat = rows // rq
    wrows = NA_WIN_TILES * rq
    win_r = min(NA_WIN_R, rows)
    col = np.arange(GRID_W)
    col_start = np.clip(col - NA_WIN_C // 2, 0, GRID_W - NA_WIN_C)
    col_ok = (col[None, :] >= col_start[:, None]) & (col[None, :] < col_start[:, None] + NA_WIN_C)
    dc = np.clip(col[None, :] - col[:, None] + NA_WIN_C - 1, 0, 2 * NA_WIN_C - 2)

    def variant(tq):
        r = tq * rq + np.arange(rq)
        kr = np.clip(tq - 1, 0, n_lat - NA_WIN_TILES) * rq + np.arange(wrows)
        rs = np.clip(r - win_r // 2, 0, rows - win_r)
        row_ok = (kr[None, :] >= rs[:, None]) & (kr[None, :] < rs[:, None] + win_r)
        dr = np.clip(kr[None, :] - r[:, None] + NA_WIN_R - 1, 0, 2 * NA_WIN_R - 2)
        ok = row_ok[:, None, :, None] & col_ok[None, :, None, :]
        shape = (rq, GRID_W, wrows, GRID_W)
        dr_f = np.broadcast_to(dr[:, None, :, None], shape).reshape(TILE, wrows * GRID_W)
        dc_f = np.broadcast_to(dc[None, :, None, :], shape).reshape(TILE, wrows * GRID_W)
        return ok.reshape(TILE, wrows * GRID_W), dr_f, dc_f

    tabs = []
    for tq in (0, 1, n_lat - 1):
        ok, dr_f, dc_f = variant(tq)
        bias = rpb[:, dr_f, dc_f].astype(F32)
        tabs.append(jnp.where(ok[None], bias, NEG_INF))
    tabs.append(jnp.full_like(tabs[0], NEG_INF))
    return jnp.stack(tabs)


def _log_sigmoid(x):
    return jnp.minimum(x, 0.0) - jnp.log(1.0 + jnp.exp(-jnp.abs(x)))


def _mlstm_kernel(q_ref, k_ref, v_ref, g_ref, o_ref, c_s, n_s, m_s):
    direction = pl.program_id(1)
    step = pl.program_id(2)

    @pl.when(step == 0)
    def _():
        c_s[...] = jnp.zeros_like(c_s)
        n_s[...] = jnp.zeros_like(n_s)
        m_s[...] = jnp.zeros_like(m_s)

    n = TILE
    sgn = 1 - 2 * direction
    row = lax.broadcasted_iota(jnp.int32, (n, n), 0)
    col = lax.broadcasted_iota(jnp.int32, (n, n), 1)
    seen = (row - col) * sgn >= 0
    seen_bf = jnp.where(seen, 1.0, 0.0).astype(BF16)
    seen_t_bf = jnp.where((col - row) * sgn >= 0, 1.0, 0.0).astype(BF16)

    gates = g_ref[...]
    logf = _log_sigmoid(gates)
    gates_t = gates.T
    lf_hi, lf_lo = _split(logf)
    cum_c = _dot(seen_bf, lf_hi) + _dot(seen_bf, lf_lo)
    lft_hi, lft_lo = _split(logf.T)
    cum_r = _dot(lft_hi, seen_t_bf) + _dot(lft_lo, seen_t_bf)
    tot = jnp.sum(logf, axis=0, keepdims=True)

    def pick(a0, a1):
        return jnp.where(direction == 0, a0, a1)

    for hd in range(ML_HEADS):
        ci, cf = hd, 4 + hd
        ig_c = pick(gates[:, ci:ci + 1], gates[:, 8 + ci:9 + ci])
        b_c = pick(cum_c[:, cf:cf + 1], cum_c[:, 8 + cf:9 + cf])
        ig_r = pick(gates_t[ci:ci + 1, :], gates_t[8 + ci:9 + ci, :])
        b_r = pick(cum_r[cf:cf + 1, :], cum_r[8 + cf:9 + cf, :])
        b_end = pick(tot[:, cf:cf + 1], tot[:, 8 + cf:9 + cf])
        m_prev = m_s[hd, 0:1, 0:1]
        n_prev = n_s[hd, 0:1, :]
        c_prev = c_s[hd]
        sl = slice(hd * ML_HEAD_DIM, (hd + 1) * ML_HEAD_DIM)
        q = q_ref[:, sl]
        k = k_ref[:, sl]
        v = v_ref[:, sl]

        a = b_c + m_prev
        dmat = jnp.where(seen, b_c - b_r + ig_r, NEG_INF)
        m_t = jnp.maximum(a, jnp.max(dmat, axis=-1, keepdims=True))
        w_inter = jnp.exp(a - m_t)
        smat = _dot_nt(q, k) * jnp.exp(dmat - m_t)
        num = w_inter * _dot(q, c_prev.astype(BF16)) + _dot(smat.astype(BF16), v)
        qn = jnp.sum(q.astype(F32) * n_prev, axis=-1, keepdims=True)
        den = w_inter * qn + jnp.sum(smat, axis=-1, keepdims=True)
        o_ref[:, sl] = (num / jnp.maximum(jnp.abs(den), jnp.exp(-m_t))).astype(BF16)

        log_w = b_end - b_c + ig_c
        m_new = jnp.maximum(b_end + m_prev, jnp.max(log_w, axis=0, keepdims=True))
        g_inter = jnp.exp(b_end + m_prev - m_new)
        kg = k.astype(F32) * jnp.exp(log_w - m_new)
        c_s[hd] = g_inter * c_prev + _dot_tn(kg.astype(BF16), v)
        n_s[hd] = jnp.broadcast_to(g_inter * n_prev + jnp.sum(kg, axis=0, keepdims=True), n_s.shape[1:])
        m_s[hd] = jnp.broadcast_to(m_new, m_s.shape[1:])


def _mlstm(cq, ck, cv, cg, nb, nt):
    t = cq.shape[0]
    w = ML_HEADS * ML_HEAD_DIM

    def tile(b, d, s):
        return b * nt + jnp.where(d == 0, s, jnp.where(s == 0, 0, nt - s))

    blk = lambda width: pl.BlockSpec((TILE, width), lambda b, d, s: (tile(b, d, s), 0))
    return pl.pallas_call(
        _mlstm_kernel,
        grid=(nb, 2, nt),
        in_specs=[blk(w), blk(w), blk(w), blk(GATE_W)],
        out_specs=pl.BlockSpec((None, TILE, w), lambda b, d, s: (d, tile(b, d, s), 0)),
        out_shape=jax.ShapeDtypeStruct((2, t, w), BF16),
        scratch_shapes=[pltpu.VMEM((ML_HEADS, ML_HEAD_DIM, ML_HEAD_DIM), F32),
                        pltpu.VMEM((ML_HEADS, 8, ML_HEAD_DIM), F32),
                        pltpu.VMEM((ML_HEADS, 8, LANES), F32)],
        compiler_params=_cparams(("arbitrary",) * 3, 32),
    )(cq, ck, cv, cg)


def _merge_kernel(x_ref, ya_ref, yb_ref, hf_ref, hb_ref, co_ref, mg_ref, mod_ref, g2_ref, gml_ref,
                  wb_ref, wout_ref, wrh_ref, wrl_ref, brt_ref, xo_ref, h2_ref, dw_ref):
    d = D_MODEL
    mod = mod_ref[...]
    hs = hf_ref[...].astype(F32) + hb_ref[...].astype(F32)
    parts = []
    for hd in range(ML_HEADS):
        v = hs[:, hd * ML_HEAD_DIM:(hd + 1) * ML_HEAD_DIM]
        parts.append(v * lax.rsqrt(jnp.mean(v * v, axis=-1, keepdims=True) + EPS))
    ym = jnp.concatenate(parts, axis=-1) * gml_ref[...] * _sigmoid(co_ref[...].astype(F32))
    merged = mg_ref[:, 0:d].astype(F32) * _dot(ya_ref[...], wb_ref[0])
    merged = merged + mg_ref[:, d:2 * d].astype(F32) * _dot(yb_ref[...], wb_ref[1])
    merged = merged + mg_ref[:, 2 * d:3 * d].astype(F32) * _dot(ym.astype(BF16), wb_ref[2])
    xn = x_ref[...] + mod[:, 2 * d:3 * d] * _dot(merged.astype(BF16), wout_ref[...])
    xo_ref[...] = xn
    h2 = _rms_mod(xn, g2_ref[...], mod[:, 3 * d:4 * d], mod[:, 4 * d:5 * d])
    h2_ref[...] = h2.astype(BF16)

    h_hi, h_lo = _split(h2)
    logits = _dot(h_hi, wrh_ref[...]) + _dot(h_hi, wrl_ref[...]) + _dot(h_lo, wrh_ref[...]) + brt_ref[...]
    lane = lax.broadcasted_iota(jnp.int32, logits.shape, 1).astype(F32)
    big = 1e9
    is_grp = (lane >= N_EXPERTS) & (lane < N_EXPERTS + N_GROUPS)
    gl = jnp.where(is_grp, logits, NEG_INF)
    gmax = jnp.max(gl, axis=-1, keepdims=True)
    gsel = jnp.min(jnp.where(gl == gmax, lane, big), axis=-1, keepdims=True) - N_EXPERTS
    p_grp = 1.0 / jnp.sum(jnp.exp(gl - gmax), axis=-1, keepdims=True)
    first = gsel * EXPERTS_PER_GROUP
    el = jnp.where((lane >= first) & (lane < first + EXPERTS_PER_GROUP), logits, NEG_INF)
    e1 = jnp.max(el, axis=-1, keepdims=True)
    i1 = jnp.min(jnp.where(el == e1, lane, big), axis=-1, keepdims=True)
    el2 = jnp.where(lane == i1, NEG_INF, el)
    e2 = jnp.max(el2, axis=-1, keepdims=True)
    i2 = jnp.min(jnp.where(el2 == e2, lane, big), axis=-1, keepdims=True)
    r = jnp.exp(e2 - e1)
    w1 = p_grp / (1.0 + r)
    dw_ref[...] = jnp.where(lane == i1, w1, 0.0) + jnp.where(lane == i2, w1 * r, 0.0)


def _merge(xs, ya, yb, hfb, co, mg, mods, g2, gml, wb, wout, wrh, wrl, brt, nb, nt):
    t = xs.shape[0]
    d = D_MODEL
    row = lambda w: pl.BlockSpec((TILE, w), lambda i: (i, 0))
    full = lambda a: pl.BlockSpec(a.shape, lambda i: (0,) * a.ndim)
    return pl.pallas_call(
        _merge_kernel,
        grid=(t // TILE,),
        in_specs=[row(d), row(512), row(512),
                  pl.BlockSpec((None, TILE, 512), lambda i: (0, i, 0)),
                  pl.BlockSpec((None, TILE, 512), lambda i: (1, i, 0)),
                  row(512), row(N_BRANCH * d),
                  pl.BlockSpec((None, 1, 6 * d), _tile_mod_index(nt, nb)),
                  full(g2), full(gml), full(wb), full(wout), full(wrh), full(wrl), full(brt)],
        out_specs=[row(d), row(d), row(LANES)],
        out_shape=[jax.ShapeDtypeStruct((t, d), F32), jax.ShapeDtypeStruct((t, d), BF16),
                   jax.ShapeDtypeStruct((t, LANES), F32)],
        compiler_params=_cparams(("arbitrary",), 48),
    )(xs, ya, yb, hfb, hfb, co, mg, mods, g2, gml, wb, wout, wrh, wrl, brt)


def _moe_kernel(h_ref, dw_ref, x_ref, modb_ref, modc_ref, wgu_ref, wd_ref, o_ref, acc_ref, *, tiles_per_batch):
    i = pl.program_id(0)
    e = pl.program_id(1)

    @pl.when(e == 0)
    def _():
        acc_ref[...] = jnp.zeros_like(acc_ref)

    gu = _dot(h_ref[...], wgu_ref[...])
    g = gu[:, :D_EXPERT]
    y = _dot((g * _sigmoid(g) * gu[:, D_EXPERT:]).astype(BF16), wd_ref[...])
    dw = dw_ref[...]
    lane = lax.broadcasted_iota(jnp.int32, dw.shape, 1)
    acc_ref[...] += jnp.sum(jnp.where(lane == e, dw, 0.0), axis=-1, keepdims=True) * y

    @pl.when(e == N_EXPERTS - 1)
    def _():
        d = D_MODEL
        rowi = lax.broadcasted_iota(jnp.int32, (x_ref.shape[0], 1), 0)
        is_ctx = (rowi < TILE) & (i % tiles_per_batch == 0)
        gate = jnp.where(is_ctx, modc_ref[:, 5 * d:6 * d], modb_ref[:, 5 * d:6 * d])
        o_ref[...] = x_ref[...] + gate * acc_ref[...]


def _moe(h2, dw, xn, mods, wgu, wd, nb, nt):
    t = h2.shape[0]
    d = D_MODEL
    tiles_per_batch = 4
    tm = nt * TILE // tiles_per_batch
    row = lambda w: pl.BlockSpec((tm, w), lambda i, e: (i, 0))
    return pl.pallas_call(
        functools.partial(_moe_kernel, tiles_per_batch=tiles_per_batch),
        grid=(t // tm, N_EXPERTS),
        in_specs=[row(d), row(LANES), row(d),
                  pl.BlockSpec((None, 1, 6 * d), lambda i, e: (i // tiles_per_batch, 0, 0)),
                  pl.BlockSpec((None, 1, 6 * d), lambda i, e: (nb, 0, 0)),
                  pl.BlockSpec((None, d, 2 * D_EXPERT), lambda i, e: (e, 0, 0)),
                  pl.BlockSpec((None, D_EXPERT, d), lambda i, e: (e, 0, 0))],
        out_specs=row(d),
        out_shape=jax.ShapeDtypeStruct((t, d), F32),
        scratch_shapes=[pltpu.VMEM((tm, d), F32)],
        compiler_params=_cparams(("arbitrary", "arbitrary"), 52),
    )(h2, dw, xn, mods, mods, wgu, wd)


def _rope_tables(n_tok, n_ctx):
    t = jnp.arange(n_tok)
    rowp = (t // GRID_W).astype(F32)
    colp = (t % GRID_W).astype(F32)
    n_freq = HEAD_DIM // 4
    inv = ROPE_THETA ** (-jnp.arange(n_freq, dtype=F32) / n_freq)
    ang = jnp.concatenate([rowp[:, None] * inv, colp[:, None] * inv], axis=-1)
    cos, sin = jnp.cos(ang), jnp.sin(ang)
    cos64 = jnp.concatenate([cos, cos], axis=-1)
    sin64 = jnp.concatenate([-sin, sin], axis=-1)
    pad = lambda a, v: jnp.concatenate([jnp.full((n_ctx, HEAD_DIM), v, F32), a], axis=0)
    return jnp.tile(pad(cos64, 1.0), (1, 4)), jnp.tile(pad(sin64, 0.0), (1, 4))


def kernel(x, c, ctx, c_ctx, w_mod, b_mod, g_norm, w_in, b_merge, g_qk, rpb, b_mlstm, g_ml, w_branch, w_out,
           w_group, b_group, w_router, b_router, w_gate_up, w_down):
    nb, n_tok, d = x.shape
    n_ctx = ctx.shape[1]
    depth = w_mod.shape[0]
    assert d == D_MODEL and n_ctx == TILE and n_tok % TILE == 0
    rows = n_tok // GRID_W
    nt = (n_ctx + n_tok) // TILE
    assert nt - 1 >= NA_WIN_TILES and rows >= NA_WIN_R
    t = nb * nt * TILE

    xs = jnp.concatenate([ctx, x], axis=1).reshape(t, d)
    n_mod = -(-(nb + 1) // 8) * 8
    cc = jnp.zeros((n_mod, d), F32).at[:nb].set(c).at[nb].set(c_ctx)
    mods_all = _modulation(cc, w_mod, b_mod)

    cos_t, sin_t = _rope_tables(n_tok, n_ctx)
    gmat = jnp.asarray(np.kron(np.eye(256 // HEAD_DIM), np.ones((HEAD_DIM, HEAD_DIM))), BF16)
    a_cols = np.concatenate([np.arange(h * HEAD_DIM, (h + 1) * HEAD_DIM) for h in _A_HEAD_ORDER])

    for l in range(depth):
        mods = mods_all[l].reshape(n_mod, 1, 6 * d)
        wl = w_in[l]
        w_in_p = jnp.concatenate(
            [wl[:, a_cols], wl[:, 512:4352], jnp.pad(wl[:, 4352:4368], ((0, 0), (0, GATE_W - 16))), wl[:, 4368:]],
            axis=1).astype(BF16)
        assert w_in_p.shape[1] == IN_WIDTH_P
        gqk = jnp.pad(jnp.tile(g_qk[l], (1, 256 // HEAD_DIM)), ((0, 4), (0, 0)))
        bcg = jnp.pad(b_mlstm[l].reshape(1, 16), ((0, 0), (0, GATE_W - 16)))
        bmg = b_merge[l].reshape(1, N_BRANCH * d)
        qa, kva, qb, kb, vb, cq, ck, cv, co, cg, mg = _in_projection(
            xs, mods, g_norm[l, 0:1], w_in_p, gmat, gqk, cos_t, sin_t, bcg, bmg, nb, nt)

        ya = _attn_a(qa, kva, nb, nt)
        yb = _attn_b(qb, kb, vb, _na_table(rpb[l], rows), nb, nt)
        hfb = _mlstm(cq, ck, cv, cg, nb, nt)

        wb = jnp.stack([w_branch[l, 0][a_cols], w_branch[l, 1], w_branch[l, 2]]).astype(BF16)
        w_rt = jnp.concatenate([w_router[l], w_group[l], jnp.zeros((d, LANES - N_EXPERTS - N_GROUPS), F32)], axis=1)
        wrh = w_rt.astype(BF16)
        wrl = (w_rt - wrh.astype(F32)).astype(BF16)
        brt = jnp.concatenate([b_router[l], b_group[l], jnp.zeros((LANES - N_EXPERTS - N_GROUPS,), F32)])[None, :]
        xn, h2, dw = _merge(xs, ya, yb, hfb, co, mg, mods, g_norm[l, 1:2], g_ml[l][None, :], wb,
                            w_out[l].astype(BF16), wrh, wrl, brt, nb, nt)
        xs = _moe(h2, dw, xn, mods, w_gate_up[l].astype(BF16), w_down[l].astype(BF16), nb, nt)

    return xs.reshape(nb, nt * TILE, d)[:, n_ctx:]
```

```python
import functools

import numpy as np
import jax
import jax.numpy as jnp
from jax import lax
from jax.experimental import pallas as pl
from jax.experimental.pallas import tpu as pltpu

D_MODEL = 1024
GRID_W = 64
HEAD_DIM = 64
ROPE_THETA = 10000.0
A_HEADS = 8
A_KV_HEADS = 2
NA_HEADS = 8
NA_WIN_R = 8
NA_WIN_C = 16
ML_HEADS = 4
ML_HEAD_DIM = 128
N_BRANCH = 3
BRANCH_W = 512
N_GROUPS = 4
EXPERTS_PER_GROUP = 8
N_EXPERTS = N_GROUPS * EXPERTS_PER_GROUP
D_EXPERT = 256
EPS = 1e-6
NEG_INF = -1e30
LOG2E = 1.4426950408889634

TILE = 256
LANES = 128
NA_WIN_TILES = 3
GATE_W = 128
V7X_VMEM_BYTES = 64 * 1024 * 1024

F32 = jnp.float32
BF16 = jnp.bfloat16

_O_QA, _O_KA, _O_VA = 0, 512, 640
_O_QB, _O_KB, _O_VB = 768, 1280, 1792
_O_CQ, _O_CK, _O_CV, _O_CO = 2304, 2816, 3328, 3840
_O_CG = 4352
_O_MG = _O_CG + GATE_W
IN_WIDTH_P = _O_MG + N_BRANCH * D_MODEL
_A_HEAD_ORDER = (0, 4, 1, 5, 2, 6, 3, 7)


def _cparams(sem, vmem_mb):
    return pltpu.CompilerParams(dimension_semantics=sem, vmem_limit_bytes=vmem_mb * 1024 * 1024)


def _dot(a, b):
    return jnp.dot(a, b, preferred_element_type=F32)


def _dot_nt(a, b):
    return lax.dot_general(a, b, (((1,), (1,)), ((), ())), preferred_element_type=F32)


def _dot_tn(a, b):
    return lax.dot_general(a, b, (((0,), (0,)), ((), ())), preferred_element_type=F32)


def _split(a):
    hi = a.astype(BF16)
    lo = (a - hi.astype(F32)).astype(BF16)
    return hi, lo


def _sigmoid(x):
    return 1.0 / (1.0 + jnp.exp(-x))


def _rms_mod(x, g, shift, scale):
    ms = jnp.mean(x * x, axis=-1, keepdims=True)
    y = x * lax.rsqrt(ms + EPS) * g
    return y * (1.0 + scale) + shift


def _mod_kernel(c_ref, w_ref, b_ref, o_ref):
    c = c_ref[...]
    a_hi, a_lo = _split(c * _sigmoid(c))
    w_hi, w_lo = _split(w_ref[...])
    o_ref[...] = _dot(a_hi, w_hi) + _dot(a_hi, w_lo) + _dot(a_lo, w_hi) + b_ref[...]


def _modulation(cc, w_mod, b_mod):
    depth, d, n = w_mod.shape
    r = cc.shape[0]
    bn = 1024
    return pl.pallas_call(
        _mod_kernel,
        grid=(depth, n // bn),
        in_specs=[
            pl.BlockSpec((r, d), lambda l, j: (0, 0)),
            pl.BlockSpec((None, d, bn), lambda l, j: (l, 0, j)),
            pl.BlockSpec((None, 1, bn), lambda l, j: (l, 0, j)),
        ],
        out_specs=pl.BlockSpec((None, r, bn), lambda l, j: (l, 0, j)),
        out_shape=jax.ShapeDtypeStruct((depth, r, n), F32),
        compiler_params=_cparams(("arbitrary", "arbitrary"), 32),
    )(cc, w_mod, b_mod.reshape(depth, 1, n))


def _group_norm64(acc, gmat, gain):
    hi, lo = _split(acc * acc)
    ss = _dot(hi, gmat) + _dot(lo, gmat)
    return acc * lax.rsqrt(ss * (1.0 / HEAD_DIM) + EPS) * gain


def _rope(x, cos, sin_signed):
    w = x.shape[1]
    lane = lax.broadcasted_iota(jnp.int32, x.shape, 1)
    nxt = pltpu.roll(x, w - HEAD_DIM // 2, 1)
    prv = pltpu.roll(x, HEAD_DIM // 2, 1)
    rot = jnp.where((lane % HEAD_DIM) < HEAD_DIM // 2, nxt, prv)
    return x * cos + rot * sin_signed


def _in_kernel(x_ref, mod_ref, g_ref, w_ref, gmat_ref, gqk_ref, cos_ref, sin_ref, bcg_ref, bmg_ref,
               qa_ref, ka_ref, vat_ref, qb_ref, kb_ref, vb_ref, cq_ref, ck_ref, cv_ref, co_ref, cg_ref, mg_ref):
    d = D_MODEL
    mod = mod_ref[...]
    h = _rms_mod(x_ref[...], g_ref[...], mod[:, 0:d], mod[:, d:2 * d]).astype(BF16)

    def proj(a, width):
        return _dot(h, w_ref[:, a:a + width])

    gmat = gmat_ref[...]
    cos = cos_ref[...]
    sin = sin_ref[...]
    att_scale = HEAD_DIM ** -0.5
    for s in range(2):
        acc = _group_norm64(proj(_O_QA + 256 * s, 256), gmat, gqk_ref[0:1, :])
        qa_ref[:, 256 * s:256 * (s + 1)] = (_rope(acc, cos, sin) * (att_scale * LOG2E)).astype(BF16)
    acc = _group_norm64(proj(_O_KA, 128), gmat[:128, :128], gqk_ref[1:2, :128])
    ka_ref[...] = _rope(acc, cos[:, :128], sin[:, :128]).astype(BF16)
    vt = proj(_O_VA, 128).T
    ones = jnp.ones((HEAD_DIM, TILE), F32)
    vat_ref[...] = jnp.concatenate([vt[:HEAD_DIM], ones, vt[HEAD_DIM:], ones], axis=0).astype(BF16)
    for s in range(2):
        acc = _group_norm64(proj(_O_QB + 256 * s, 256), gmat, gqk_ref[2:3, :])
        qb_ref[:, 256 * s:256 * (s + 1)] = (acc * att_scale).astype(BF16)
        acc = _group_norm64(proj(_O_KB + 256 * s, 256), gmat, gqk_ref[3:4, :])
        kb_ref[:, 256 * s:256 * (s + 1)] = acc.astype(BF16)
    vb_ref[...] = proj(_O_VB, 512).astype(BF16)
    cq_ref[...] = proj(_O_CQ, 512).astype(BF16)
    ck_ref[...] = (proj(_O_CK, 512) * (ML_HEAD_DIM ** -0.5)).astype(BF16)
    cv_ref[...] = proj(_O_CV, 512).astype(BF16)
    co_ref[...] = proj(_O_CO, 512).astype(BF16)
    cg_ref[...] = proj(_O_CG, GATE_W) + bcg_ref[...]
    for s in range(N_BRANCH * D_MODEL // 512):
        a = 512 * s
        mg_ref[:, a:a + 512] = _sigmoid(proj(_O_MG + a, 512) + bmg_ref[:, a:a + 512]).astype(BF16)


def _tile_mod_index(nt, nb):
    return lambda i: (jnp.where(i % nt == 0, nb, i // nt), 0, 0)


def _in_projection(xs, mods, g1, w_in_p, gmat, gqk, cos_t, sin_t, bcg, bmg, nb, nt):
    t = xs.shape[0]
    d = D_MODEL
    row = lambda w: pl.BlockSpec((TILE, w), lambda i: (i, 0))
    full = lambda a: pl.BlockSpec(a.shape, lambda i: (0,) * a.ndim)
    widths = (512, 512, 512, 512, 512, 512, 512)
    out_shape = [jax.ShapeDtypeStruct((t, 512), BF16), jax.ShapeDtypeStruct((t, LANES), BF16),
                 jax.ShapeDtypeStruct((nb, 2 * LANES, nt * TILE), BF16)]
    out_shape += [jax.ShapeDtypeStruct((t, w), BF16) for w in widths]
    out_shape += [jax.ShapeDtypeStruct((t, GATE_W), F32), jax.ShapeDtypeStruct((t, N_BRANCH * d), BF16)]
    out_specs = [row(512), row(LANES), pl.BlockSpec((None, 2 * LANES, TILE), lambda i: (i // nt, 0, i % nt))]
    out_specs += [row(w) for w in widths] + [row(GATE_W), row(N_BRANCH * d)]
    return pl.pallas_call(
        _in_kernel,
        grid=(t // TILE,),
        in_specs=[
            row(d),
            pl.BlockSpec((None, 1, 6 * d), _tile_mod_index(nt, nb)),
            full(g1),
            pl.BlockSpec(w_in_p.shape, lambda i: (0, 0), pipeline_mode=pl.Buffered(1)),
            full(gmat),
            full(gqk),
            pl.BlockSpec((TILE, 256), lambda i: (i % nt, 0)),
            pl.BlockSpec((TILE, 256), lambda i: (i % nt, 0)),
            full(bcg),
            full(bmg),
        ],
        out_specs=out_specs,
        out_shape=out_shape,
        compiler_params=_cparams(("arbitrary",), 52),
    )(xs, mods, g1, w_in_p, gmat, gqk, cos_t, sin_t, bcg, bmg)


def _attn_a_kernel(q_ref, k_ref, vt_ref, o_ref, s_scr):
    qt = pl.program_id(1)
    left = lax.broadcasted_iota(jnp.int32, (TILE, LANES), 1) < HEAD_DIM

    def query(hd):
        qblk = q_ref[:, (hd // 2) * LANES:(hd // 2 + 1) * LANES]
        zero = jnp.zeros_like(qblk)
        return jnp.where(left, qblk, zero) if hd % 2 == 0 else jnp.where(left, zero, qblk)

    def fold8(a):
        return functools.reduce(jnp.maximum, [a[8 * i:8 * (i + 1), :] for i in range(TILE // 8)])

    def run(n_chunks):
        rows = lambda c: slice(c * TILE, (c + 1) * TILE)

        def scores(hd, qh, c, m8):
            st = _dot_nt(k_ref[rows(c), :], qh)
            s_scr[hd % 2, rows(c), :] = st
            return jnp.maximum(m8, fold8(st))

        def values(hd, c, m, acc):
            p = jnp.exp2(s_scr[hd % 2, rows(c), :] - m).astype(BF16)
            kv = hd % 2
            return acc + _dot(vt_ref[kv * LANES:(kv + 1) * LANES, rows(c)], p)

        neg = jnp.full((8, TILE), NEG_INF, F32)
        qh = query(0)
        m8 = neg
        for c in range(n_chunks):
            m8 = scores(0, qh, c, m8)
        for hd in range(A_HEADS):
            m = jnp.max(m8, axis=0, keepdims=True)
            acc = jnp.zeros((LANES, TILE), F32)
            m8 = neg
            qh = query(hd + 1) if hd + 1 < A_HEADS else None
            for c in range(n_chunks):
                acc = values(hd, c, m, acc)
                if qh is not None:
                    m8 = scores(hd + 1, qh, c, m8)
            o_ref[hd * HEAD_DIM:(hd + 1) * HEAD_DIM, :] = (
                acc[0:HEAD_DIM, :] / acc[HEAD_DIM:HEAD_DIM + 1, :]).astype(BF16)

    @pl.when(qt == 0)
    def _():
        run(1)

    @pl.when(qt > 0)
    def _():
        run(k_ref.shape[0] // TILE)


def _attn_a(qa, ka, vat, nb, nt):
    s = nt * TILE
    w = A_HEADS * HEAD_DIM
    return pl.pallas_call(
        _attn_a_kernel,
        grid=(nb, nt),
        in_specs=[
            pl.BlockSpec((TILE, w), lambda b, i: (b * nt + i, 0)),
            pl.BlockSpec((None, s, LANES), lambda b, i: (b, 0, 0)),
            pl.BlockSpec((None, 2 * LANES, s), lambda b, i: (b, 0, 0)),
        ],
        out_specs=pl.BlockSpec((None, w, TILE), lambda b, i: (b, 0, i)),
        out_shape=jax.ShapeDtypeStruct((nb, w, s), BF16),
        scratch_shapes=[pltpu.VMEM((2, s, TILE), F32)],
        compiler_params=_cparams(("arbitrary",) * 2, 48),
    )(qa, ka.reshape(nb, s, LANES), vat)


def _attn_b_kernel(q_ref, k0_ref, k1_ref, k2_ref, kc_ref, v0_ref, v1_ref, v2_ref, vc_ref, tab_ref, o_ref):
    left = lax.broadcasted_iota(jnp.int32, (TILE, LANES), 1) < HEAD_DIM
    k_refs = (k0_ref, k1_ref, k2_ref, kc_ref)
    v_refs = (v0_ref, v1_ref, v2_ref, vc_ref)
    for p in range(NA_HEADS // 2):
        sl = slice(p * LANES, (p + 1) * LANES)
        qblk = q_ref[:, sl]
        ks = [r[:, sl] for r in k_refs]
        vs = [r[:, sl] for r in v_refs]
        outs = []
        for half in range(2):
            hd = 2 * p + half
            qh = jnp.where(left if half == 0 else jnp.logical_not(left), qblk, jnp.zeros_like(qblk))
            ss = [_dot_nt(qh, kk) for kk in ks]
            for j in range(NA_WIN_TILES):
                ss[j] = ss[j] + tab_ref[hd, :, j * TILE:(j + 1) * TILE]
            m = functools.reduce(jnp.maximum, [jnp.max(s, axis=-1, keepdims=True) for s in ss])
            ps = [jnp.exp(s - m) for s in ss]
            l = functools.reduce(jnp.add, [jnp.sum(pp, axis=-1, keepdims=True) for pp in ps])
            o = functools.reduce(jnp.add, [_dot(pp.astype(BF16), vv) for pp, vv in zip(ps, vs)])
            outs.append(o / l)
        o_ref[:, sl] = jnp.where(left, outs[0], outs[1]).astype(BF16)


def _attn_b(qb, kb, vb, table, nb, nt):
    t = qb.shape[0]
    w = NA_HEADS * HEAD_DIM
    n_lat = nt - 1

    def win(j):
        return lambda i, b: (b * nt + 1 + jnp.clip(i - 2, 0, n_lat - NA_WIN_TILES) + j, 0)

    ctx = lambda i, b: (b * nt, 0)
    variant = lambda i, b: (jnp.where(i == 0, 3, jnp.where(i == 1, 0, jnp.where(i == n_lat, 2, 1))), 0, 0, 0)
    blk = lambda f: pl.BlockSpec((TILE, w), f)
    return pl.pallas_call(
        _attn_b_kernel,
        grid=(nt, nb),
        in_specs=[blk(lambda i, b: (b * nt + i, 0)),
                  blk(win(0)), blk(win(1)), blk(win(2)), blk(ctx),
                  blk(win(0)), blk(win(1)), blk(win(2)), blk(ctx),
                  pl.BlockSpec((None, NA_HEADS, TILE, NA_WIN_TILES * TILE), variant)],
        out_specs=blk(lambda i, b: (b * nt + i, 0)),
        out_shape=jax.ShapeDtypeStruct((t, w), BF16),
        compiler_params=_cparams(("arbitrary", "arbitrary"), 48),
    )(qb, kb, kb, kb, kb, vb, vb, vb, vb, table)


def _na_table(rpb, rows):
    rq = TILE // GRID_W
    n_lat = rows // rq
    wrows = NA_WIN_TILES * rq
    win_r = min(NA_WIN_R, rows)
    col = np.arange(GRID_W)
    col_start = np.clip(col - NA_WIN_C // 2, 0, GRID_W - NA_WIN_C)
    col_ok = (col[None, :] >= col_start[:, None]) & (col[None, :] < col_start[:, None] + NA_WIN_C)
    assert np.all(np.abs(col[None, :] - col[:, None])[col_ok] < NA_WIN_C)

    nh, n_dr, n_dc = rpb.shape
    lpad = GRID_W - NA_WIN_C
    v = jnp.pad(rpb.astype(F32), ((0, 0), (0, 0), (lpad, 2 * GRID_W - lpad - n_dc)))
    skew = jnp.tile(v, (1, 1, GRID_W))[:, :, :GRID_W * (2 * GRID_W - 1)].reshape(nh, n_dr, GRID_W, 2 * GRID_W - 1)
    toep = skew[:, :, :, GRID_W - 1:]

    tabs = []
    for tq in (0, 1, n_lat - 1):
        r = tq * rq + np.arange(rq)
        kr = np.clip(tq - 1, 0, n_lat - NA_WIN_TILES) * rq + np.arange(wrows)
        rs = np.clip(r - win_r // 2, 0, rows - win_r)
        row_ok = (kr[None, :] >= rs[:, None]) & (kr[None, :] < rs[:, None] + win_r)
        dr = np.clip(kr[None, :] - r[:, None] + NA_WIN_R - 1, 0, 2 * NA_WIN_R - 2)
        ok = (row_ok[:, None, :, None] & col_ok[None, :, None, :]).reshape(TILE, wrows * GRID_W)
        blocks = jnp.stack([toep[:, a] for a in dr.ravel()], axis=1)
        bias = blocks.reshape(nh, rq, wrows, GRID_W, GRID_W).transpose(0, 1, 3, 2, 4)
        tabs.append(jnp.where(ok[None], bias.reshape(nh, TILE, wrows * GRID_W), NEG_INF))
    tabs.append(jnp.full_like(tabs[0], NEG_INF))
    return jnp.stack(tabs)


def _log_sigmoid(x):
    return jnp.minimum(x, 0.0) - jnp.log(1.0 + jnp.exp(-jnp.abs(x)))


def _mlstm_kernel(q_ref, k_ref, v_ref, g_ref, o_ref, c_s, n_s, m_s):
    direction = pl.program_id(1)
    step = pl.program_id(2)

    @pl.when(step == 0)
    def _():
        c_s[...] = jnp.zeros_like(c_s)
        n_s[...] = jnp.zeros_like(n_s)
        m_s[...] = jnp.zeros_like(m_s)

    n = TILE
    sgn = 1 - 2 * direction
    row = lax.broadcasted_iota(jnp.int32, (n, n), 0)
    col = lax.broadcasted_iota(jnp.int32, (n, n), 1)
    seen = (row - col) * sgn >= 0
    seen_bf = jnp.where(seen, 1.0, 0.0).astype(BF16)
    seen_t_bf = jnp.where((col - row) * sgn >= 0, 1.0, 0.0).astype(BF16)

    gates = g_ref[...]
    logf = _log_sigmoid(gates)
    gates_t = gates.T
    lf_hi, lf_lo = _split(logf)
    cum_c = _dot(seen_bf, lf_hi) + _dot(seen_bf, lf_lo)
    lft_hi, lft_lo = _split(logf.T)
    cum_r = _dot(lft_hi, seen_t_bf) + _dot(lft_lo, seen_t_bf)
    tot = jnp.sum(logf, axis=0, keepdims=True)

    def pick(a0, a1):
        return jnp.where(direction == 0, a0, a1)

    for hd in range(ML_HEADS):
        ci, cf = hd, 4 + hd
        ig_c = pick(gates[:, ci:ci + 1], gates[:, 8 + ci:9 + ci])
        b_c = pick(cum_c[:, cf:cf + 1], cum_c[:, 8 + cf:9 + cf])
        ig_r = pick(gates_t[ci:ci + 1, :], gates_t[8 + ci:9 + ci, :])
        b_r = pick(cum_r[cf:cf + 1, :], cum_r[8 + cf:9 + cf, :])
        b_end = pick(tot[:, cf:cf + 1], tot[:, 8 + cf:9 + cf])
        m_prev = m_s[hd, 0:1, 0:1]
        n_prev = n_s[hd, 0:1, :]
        c_prev = c_s[hd]
        sl = slice(hd * ML_HEAD_DIM, (hd + 1) * ML_HEAD_DIM)
        q = q_ref[:, sl]
        k = k_ref[:, sl]
        v = v_ref[:, sl]

        a = b_c + m_prev
        dmat = jnp.where(seen, b_c - b_r + ig_r, NEG_INF)
        m_t = jnp.maximum(a, jnp.max(dmat, axis=-1, keepdims=True))
        w_inter = jnp.exp(a - m_t)
        smat = _dot_nt(q, k) * jnp.exp(dmat - m_t)
        num = w_inter * _dot(q, c_prev.astype(BF16)) + _dot(smat.astype(BF16), v)
        qn = jnp.sum(q.astype(F32) * n_prev, axis=-1, keepdims=True)
        den = w_inter * qn + jnp.sum(smat, axis=-1, keepdims=True)
        o_ref[:, sl] = (num / jnp.maximum(jnp.abs(den), jnp.exp(-m_t))).astype(BF16)

        log_w = b_end - b_c + ig_c
        m_new = jnp.maximum(b_end + m_prev, jnp.max(log_w, axis=0, keepdims=True))
        g_inter = jnp.exp(b_end + m_prev - m_new)
        kg = k.astype(F32) * jnp.exp(log_w - m_new)
        c_s[hd] = g_inter * c_prev + _dot_tn(kg.astype(BF16), v)
        n_s[hd] = jnp.broadcast_to(g_inter * n_prev + jnp.sum(kg, axis=0, keepdims=True), n_s.shape[1:])
        m_s[hd] = jnp.broadcast_to(m_new, m_s.shape[1:])


def _mlstm(cq, ck, cv, cg, nb, nt):
    t = cq.shape[0]
    w = ML_HEADS * ML_HEAD_DIM

    def tile(b, d, s):
        return b * nt + jnp.where(d == 0, s, jnp.where(s == 0, 0, nt - s))

    blk = lambda width: pl.BlockSpec((TILE, width), lambda b, d, s: (tile(b, d, s), 0))
    return pl.pallas_call(
        _mlstm_kernel,
        grid=(nb, 2, nt),
        in_specs=[blk(w), blk(w), blk(w), blk(GATE_W)],
        out_specs=pl.BlockSpec((None, TILE, w), lambda b, d, s: (d, tile(b, d, s), 0)),
        out_shape=jax.ShapeDtypeStruct((2, t, w), BF16),
        scratch_shapes=[pltpu.VMEM((ML_HEADS, ML_HEAD_DIM, ML_HEAD_DIM), F32),
                        pltpu.VMEM((ML_HEADS, 8, ML_HEAD_DIM), F32),
                        pltpu.VMEM((ML_HEADS, 8, LANES), F32)],
        compiler_params=_cparams(("arbitrary",) * 3, 32),
    )(cq, ck, cv, cg)


def _merge_kernel(x_ref, ya_ref, yb_ref, hf_ref, hb_ref, co_ref, mg_ref, mod_ref, g2_ref, gml_ref,
                  wb_ref, wout_ref, wrh_ref, wrl_ref, brt_ref, xo_ref, h2_ref, dw_ref):
    d = D_MODEL
    mod = mod_ref[...]
    hs = hf_ref[...].astype(F32) + hb_ref[...].astype(F32)
    parts = []
    for hd in range(ML_HEADS):
        v = hs[:, hd * ML_HEAD_DIM:(hd + 1) * ML_HEAD_DIM]
        parts.append(v * lax.rsqrt(jnp.mean(v * v, axis=-1, keepdims=True) + EPS))
    ym = jnp.concatenate(parts, axis=-1) * gml_ref[...] * _sigmoid(co_ref[...].astype(F32))
    merged = mg_ref[:, 0:d].astype(F32) * _dot_tn(ya_ref[...], wb_ref[0])
    merged = merged + mg_ref[:, d:2 * d].astype(F32) * _dot(yb_ref[...], wb_ref[1])
    merged = merged + mg_ref[:, 2 * d:3 * d].astype(F32) * _dot(ym.astype(BF16), wb_ref[2])
    xn = x_ref[...] + mod[:, 2 * d:3 * d] * _dot(merged.astype(BF16), wout_ref[...])
    xo_ref[...] = xn
    h2 = _rms_mod(xn, g2_ref[...], mod[:, 3 * d:4 * d], mod[:, 4 * d:5 * d])
    h2_ref[...] = h2.astype(BF16)

    h_hi, h_lo = _split(h2)
    logits = _dot(h_hi, wrh_ref[...]) + _dot(h_hi, wrl_ref[...]) + _dot(h_lo, wrh_ref[...]) + brt_ref[...]
    lane = lax.broadcasted_iota(jnp.int32, logits.shape, 1).astype(F32)
    big = 1e9
    is_grp = (lane >= N_EXPERTS) & (lane < N_EXPERTS + N_GROUPS)
    gl = jnp.where(is_grp, logits, NEG_INF)
    gmax = jnp.max(gl, axis=-1, keepdims=True)
    gsel = jnp.min(jnp.where(gl == gmax, lane, big), axis=-1, keepdims=True) - N_EXPERTS
    p_grp = 1.0 / jnp.sum(jnp.exp(gl - gmax), axis=-1, keepdims=True)
    first = gsel * EXPERTS_PER_GROUP
    el = jnp.where((lane >= first) & (lane < first + EXPERTS_PER_GROUP), logits, NEG_INF)
    e1 = jnp.max(el, axis=-1, keepdims=True)
    i1 = jnp.min(jnp.where(el == e1, lane, big), axis=-1, keepdims=True)
    el2 = jnp.where(lane == i1, NEG_INF, el)
    e2 = jnp.max(el2, axis=-1, keepdims=True)
    i2 = jnp.min(jnp.where(el2 == e2, lane, big), axis=-1, keepdims=True)
    r = jnp.exp(e2 - e1)
    w1 = p_grp / (1.0 + r)
    dw_ref[...] = jnp.where(lane == i1, w1, 0.0) + jnp.where(lane == i2, w1 * r, 0.0)


def _merge(xs, ya, yb, hfb, co, mg, mods, g2, gml, wb, wout, wrh, wrl, brt, nb, nt):
    t = xs.shape[0]
    d = D_MODEL
    row = lambda w: pl.BlockSpec((TILE, w), lambda i: (i, 0))
    full = lambda a: pl.BlockSpec(a.shape, lambda i: (0,) * a.ndim)
    return pl.pallas_call(
        _merge_kernel,
        grid=(t // TILE,),
        in_specs=[row(d), pl.BlockSpec((None, 512, TILE), lambda i: (i // nt, 0, i % nt)), row(512),
                  pl.BlockSpec((None, TILE, 512), lambda i: (0, i, 0)),
                  pl.BlockSpec((None, TILE, 512), lambda i: (1, i, 0)),
                  row(512), row(N_BRANCH * d),
                  pl.BlockSpec((None, 1, 6 * d), _tile_mod_index(nt, nb)),
                  full(g2), full(gml), full(wb), full(wout), full(wrh), full(wrl), full(brt)],
        out_specs=[row(d), row(d), row(LANES)],
        out_shape=[jax.ShapeDtypeStruct((t, d), F32), jax.ShapeDtypeStruct((t, d), BF16),
                   jax.ShapeDtypeStruct((t, LANES), F32)],
        compiler_params=_cparams(("arbitrary",), 48),
    )(xs, ya, yb, hfb, hfb, co, mg, mods, g2, gml, wb, wout, wrh, wrl, brt)


def _moe_kernel(h_ref, dw_ref, x_ref, modb_ref, modc_ref, wgu_ref, wd_ref, o_ref, acc_ref, *, tiles_per_batch):
    i = pl.program_id(0)
    e = pl.program_id(1)

    @pl.when(e == 0)
    def _():
        acc_ref[...] = jnp.zeros_like(acc_ref)

    gu = _dot(h_ref[...], wgu_ref[...])
    g = gu[:, :D_EXPERT]
    y = _dot((g * _sigmoid(g) * gu[:, D_EXPERT:]).astype(BF16), wd_ref[...])
    dw = dw_ref[...]
    lane = lax.broadcasted_iota(jnp.int32, dw.shape, 1)
    acc_ref[...] += jnp.sum(jnp.where(lane == e, dw, 0.0), axis=-1, keepdims=True) * y

    @pl.when(e == N_EXPERTS - 1)
    def _():
        d = D_MODEL
        rowi = lax.broadcasted_iota(jnp.int32, (x_ref.shape[0], 1), 0)
        is_ctx = (rowi < TILE) & (i % tiles_per_batch == 0)
        gate = jnp.where(is_ctx, modc_ref[:, 5 * d:6 * d], modb_ref[:, 5 * d:6 * d])
        o_ref[...] = x_ref[...] + gate * acc_ref[...]


def _moe(h2, dw, xn, mods, wgu, wd, nb, nt):
    t = h2.shape[0]
    d = D_MODEL
    tiles_per_batch = 4
    tm = nt * TILE // tiles_per_batch
    row = lambda w: pl.BlockSpec((tm, w), lambda i, e: (i, 0))
    return pl.pallas_call(
        functools.partial(_moe_kernel, tiles_per_batch=tiles_per_batch),
        grid=(t // tm, N_EXPERTS),
        in_specs=[row(d), row(LANES), row(d),
                  pl.BlockSpec((None, 1, 6 * d), lambda i, e: (i // tiles_per_batch, 0, 0)),
                  pl.BlockSpec((None, 1, 6 * d), lambda i, e: (nb, 0, 0)),
                  pl.BlockSpec((None, d, 2 * D_EXPERT), lambda i, e: (e, 0, 0)),
                  pl.BlockSpec((None, D_EXPERT, d), lambda i, e: (e, 0, 0))],
        out_specs=row(d),
        out_shape=jax.ShapeDtypeStruct((t, d), F32),
        scratch_shapes=[pltpu.VMEM((tm, d), F32)],
        compiler_params=_cparams(("arbitrary", "arbitrary"), 52),
    )(h2, dw, xn, mods, mods, wgu, wd)


def _rope_tables(n_tok, n_ctx):
    t = jnp.arange(n_tok)
    rowp = (t // GRID_W).astype(F32)
    colp = (t % GRID_W).astype(F32)
    n_freq = HEAD_DIM // 4
    inv = ROPE_THETA ** (-jnp.arange(n_freq, dtype=F32) / n_freq)
    ang = jnp.concatenate([rowp[:, None] * inv, colp[:, None] * inv], axis=-1)
    cos, sin = jnp.cos(ang), jnp.sin(ang)
    cos64 = jnp.concatenate([cos, cos], axis=-1)
    sin64 = jnp.concatenate([-sin, sin], axis=-1)
    pad = lambda a, v: jnp.concatenate([jnp.full((n_ctx, HEAD_DIM), v, F32), a], axis=0)
    return jnp.tile(pad(cos64, 1.0), (1, 4)), jnp.tile(pad(sin64, 0.0), (1, 4))


def kernel(x, c, ctx, c_ctx, w_mod, b_mod, g_norm, w_in, b_merge, g_qk, rpb, b_mlstm, g_ml, w_branch, w_out,
           w_group, b_group, w_router, b_router, w_gate_up, w_down):
    nb, n_tok, d = x.shape
    n_ctx = ctx.shape[1]
    depth = w_mod.shape[0]
    assert d == D_MODEL and n_ctx == TILE and n_tok % TILE == 0
    rows = n_tok // GRID_W
    nt = (n_ctx + n_tok) // TILE
    assert nt - 1 >= NA_WIN_TILES and rows >= NA_WIN_R
    t = nb * nt * TILE

    xs = jnp.concatenate([ctx, x], axis=1).reshape(t, d)
    n_mod = -(-(nb + 1) // 8) * 8
    cc = jnp.zeros((n_mod, d), F32).at[:nb].set(c).at[nb].set(c_ctx)
    mods_all = _modulation(cc, w_mod, b_mod)

    cos_t, sin_t = _rope_tables(n_tok, n_ctx)
    gmat = jnp.asarray(np.kron(np.eye(256 // HEAD_DIM), np.ones((HEAD_DIM, HEAD_DIM))), BF16)
    a_cols = np.concatenate([np.arange(h * HEAD_DIM, (h + 1) * HEAD_DIM) for h in _A_HEAD_ORDER])

    for l in range(depth):
        mods = mods_all[l].reshape(n_mod, 1, 6 * d)
        wl = w_in[l]
        w_in_p = jnp.concatenate(
            [wl[:, a_cols], wl[:, 512:4352], jnp.pad(wl[:, 4352:4368], ((0, 0), (0, GATE_W - 16))), wl[:, 4368:]],
            axis=1).astype(BF16)
        assert w_in_p.shape[1] == IN_WIDTH_P
        gqk = jnp.pad(jnp.tile(g_qk[l], (1, 256 // HEAD_DIM)), ((0, 4), (0, 0)))
        bcg = jnp.pad(b_mlstm[l].reshape(1, 16), ((0, 0), (0, GATE_W - 16)))
        bmg = b_merge[l].reshape(1, N_BRANCH * d)
        qa, ka, vat, qb, kb, vb, cq, ck, cv, co, cg, mg = _in_projection(
            xs, mods, g_norm[l, 0:1], w_in_p, gmat, gqk, cos_t, sin_t, bcg, bmg, nb, nt)

        ya = _attn_a(qa, ka, vat, nb, nt)
        yb = _attn_b(qb, kb, vb, _na_table(rpb[l], rows), nb, nt)
        hfb = _mlstm(cq, ck, cv, cg, nb, nt)

        wb = jnp.stack([w_branch[l, 0][a_cols], w_branch[l, 1], w_branch[l, 2]]).astype(BF16)
        w_rt = jnp.concatenate([w_router[l], w_group[l], jnp.zeros((d, LANES - N_EXPERTS - N_GROUPS), F32)], axis=1)
        wrh = w_rt.astype(BF16)
        wrl = (w_rt - wrh.astype(F32)).astype(BF16)
        brt = jnp.concatenate([b_router[l], b_group[l], jnp.zeros((LANES - N_EXPERTS - N_GROUPS,), F32)])[None, :]
        xn, h2, dw = _merge(xs, ya, yb, hfb, co, mg, mods, g_norm[l, 1:2], g_ml[l][None, :], wb,
                            w_out[l].astype(BF16), wrh, wrl, brt, nb, nt)
        xs = _moe(h2, dw, xn, mods, w_gate_up[l].astype(BF16), w_down[l].astype(BF16), nb, nt)

    return xs.reshape(nb, nt * TILE, d)[:, n_ctx:]
```

```python
import functools

import numpy as np
import jax
import jax.numpy as jnp
from jax import lax
from jax.experimental import pallas as pl
from jax.experimental.pallas import tpu as pltpu

D_MODEL = 1024
GRID_W = 64
HEAD_DIM = 64
ROPE_THETA = 10000.0
A_HEADS = 8
A_KV_HEADS = 2
NA_HEADS = 8
NA_WIN_R = 8
NA_WIN_C = 16
ML_HEADS = 4
ML_HEAD_DIM = 128
N_BRANCH = 3
BRANCH_W = 512
N_GROUPS = 4
EXPERTS_PER_GROUP = 8
N_EXPERTS = N_GROUPS * EXPERTS_PER_GROUP
D_EXPERT = 256
EPS = 1e-6
NEG_INF = -1e30
LOG2E = 1.4426950408889634

TILE = 256
LANES = 128
NA_WIN_TILES = 3
GATE_W = 128
V7X_VMEM_BYTES = 64 * 1024 * 1024

F32 = jnp.float32
BF16 = jnp.bfloat16

_O_QA, _O_KA, _O_VA = 0, 512, 640
_O_QB, _O_KB, _O_VB = 768, 1280, 1792
_O_CQ, _O_CK, _O_CV, _O_CO = 2304, 2816, 3328, 3840
_O_CG = 4352
_O_MG = _O_CG + GATE_W
IN_WIDTH_P = _O_MG + N_BRANCH * D_MODEL
_A_HEAD_ORDER = (0, 4, 1, 5, 2, 6, 3, 7)


def _cparams(sem, vmem_mb):
    return pltpu.CompilerParams(dimension_semantics=sem, vmem_limit_bytes=vmem_mb * 1024 * 1024)


def _dot(a, b):
    return jnp.dot(a, b, preferred_element_type=F32)


def _dot_nt(a, b):
    return lax.dot_general(a, b, (((1,), (1,)), ((), ())), preferred_element_type=F32)


def _dot_tn(a, b):
    return lax.dot_general(a, b, (((0,), (0,)), ((), ())), preferred_element_type=F32)


def _split(a):
    hi = a.astype(BF16)
    lo = (a - hi.astype(F32)).astype(BF16)
    return hi, lo


def _sigmoid(x):
    return 1.0 / (1.0 + jnp.exp(-x))


def _rms_mod(x, g, shift, scale):
    ms = jnp.mean(x * x, axis=-1, keepdims=True)
    y = x * lax.rsqrt(ms + EPS) * g
    return y * (1.0 + scale) + shift


def _mod_kernel(c_ref, w_ref, b_ref, o_ref):
    c = c_ref[...]
    a_hi, a_lo = _split(c * _sigmoid(c))
    w_hi, w_lo = _split(w_ref[...])
    o_ref[...] = _dot(a_hi, w_hi) + _dot(a_hi, w_lo) + _dot(a_lo, w_hi) + b_ref[...]


def _modulation(cc, w_mod, b_mod):
    depth, d, n = w_mod.shape
    r = cc.shape[0]
    bn = 1024
    return pl.pallas_call(
        _mod_kernel,
        grid=(depth, n // bn),
        in_specs=[
            pl.BlockSpec((r, d), lambda l, j: (0, 0)),
            pl.BlockSpec((None, d, bn), lambda l, j: (l, 0, j)),
            pl.BlockSpec((None, 1, bn), lambda l, j: (l, 0, j)),
        ],
        out_specs=pl.BlockSpec((None, r, bn), lambda l, j: (l, 0, j)),
        out_shape=jax.ShapeDtypeStruct((depth, r, n), F32),
        compiler_params=_cparams(("arbitrary", "arbitrary"), 32),
    )(cc, w_mod, b_mod.reshape(depth, 1, n))


def _group_norm64(acc, gmat, gain):
    hi, lo = _split(acc * acc)
    ss = _dot(hi, gmat) + _dot(lo, gmat)
    return acc * lax.rsqrt(ss * (1.0 / HEAD_DIM) + EPS) * gain


def _rope(x, cos, sin_signed):
    w = x.shape[1]
    lane = lax.broadcasted_iota(jnp.int32, x.shape, 1)
    nxt = pltpu.roll(x, w - HEAD_DIM // 2, 1)
    prv = pltpu.roll(x, HEAD_DIM // 2, 1)
    rot = jnp.where((lane % HEAD_DIM) < HEAD_DIM // 2, nxt, prv)
    return x * cos + rot * sin_signed


def _in_kernel(*refs, has_moe):
    d = D_MODEL
    if has_moe:
        x_ref, y_ref, modp_ref, *refs = refs
    else:
        x_ref, *refs = refs
    mod_ref, g_ref, w_ref, gmat_ref, gqk_ref, cos_ref, sin_ref, bcg_ref, bmg_ref, *outs = refs
    x = x_ref[...]
    if has_moe:
        xo_ref, *outs = outs
        x = x + modp_ref[:, 5 * d:6 * d] * y_ref[...].astype(F32)
        xo_ref[...] = x
    qa_ref, ka_ref, vat_ref, qb_ref, kb_ref, vb_ref, cq_ref, ck_ref, cv_ref, co_ref, cg_ref, mg_ref = outs
    mod = mod_ref[...]
    h = _rms_mod(x, g_ref[...], mod[:, 0:d], mod[:, d:2 * d]).astype(BF16)

    def proj(a, width):
        return _dot(h, w_ref[:, a:a + width])

    gmat = gmat_ref[...]
    cos = cos_ref[...]
    sin = sin_ref[...]
    att_scale = HEAD_DIM ** -0.5
    for s in range(2):
        acc = _group_norm64(proj(_O_QA + 256 * s, 256), gmat, gqk_ref[0:1, :])
        qa_ref[:, 256 * s:256 * (s + 1)] = (_rope(acc, cos, sin) * (att_scale * LOG2E)).astype(BF16)
    acc = _group_norm64(proj(_O_KA, 128), gmat[:128, :128], gqk_ref[1:2, :128])
    ka_ref[...] = _rope(acc, cos[:, :128], sin[:, :128]).astype(BF16)
    vt = proj(_O_VA, 128).T
    ones = jnp.ones((HEAD_DIM, TILE), F32)
    vat_ref[...] = jnp.concatenate([vt[:HEAD_DIM], ones, vt[HEAD_DIM:], ones], axis=0).astype(BF16)
    for s in range(2):
        acc = _group_norm64(proj(_O_QB + 256 * s, 256), gmat, gqk_ref[2:3, :])
        qb_ref[:, 256 * s:256 * (s + 1)] = (acc * att_scale).astype(BF16)
        acc = _group_norm64(proj(_O_KB + 256 * s, 256), gmat, gqk_ref[3:4, :])
        kb_ref[:, 256 * s:256 * (s + 1)] = acc.astype(BF16)
    vb_ref[...] = proj(_O_VB, 512).astype(BF16)
    cq_ref[...] = proj(_O_CQ, 512).astype(BF16)
    ck_ref[...] = (proj(_O_CK, 512) * (ML_HEAD_DIM ** -0.5)).astype(BF16)
    cv_ref[...] = proj(_O_CV, 512).astype(BF16)
    co_ref[...] = proj(_O_CO, 512).astype(BF16)
    cg_ref[...] = proj(_O_CG, GATE_W) + bcg_ref[...]
    for s in range(N_BRANCH * D_MODEL // 512):
        a = 512 * s
        mg_ref[:, a:a + 512] = _sigmoid(proj(_O_MG + a, 512) + bmg_ref[:, a:a + 512]).astype(BF16)


def _tile_mod_index(nt, nb):
    return lambda i: (jnp.where(i % nt == 0, nb, i // nt), 0, 0)


def _in_projection(xs, moe_prev, mods_prev, mods, g1, w_in_p, gmat, gqk, cos_t, sin_t, bcg, bmg, nb, nt):
    t = xs.shape[0]
    d = D_MODEL
    has_moe = moe_prev is not None
    row = lambda w: pl.BlockSpec((TILE, w), lambda i: (i, 0))
    full = lambda a: pl.BlockSpec(a.shape, lambda i: (0,) * a.ndim)
    mod_spec = pl.BlockSpec((None, 1, 6 * d), _tile_mod_index(nt, nb))
    widths = (512, 512, 512, 512, 512, 512, 512)
    out_shape = [jax.ShapeDtypeStruct((t, 512), BF16), jax.ShapeDtypeStruct((t, LANES), BF16),
                 jax.ShapeDtypeStruct((nb, 2 * LANES, nt * TILE), BF16)]
    out_shape += [jax.ShapeDtypeStruct((t, w), BF16) for w in widths]
    out_shape += [jax.ShapeDtypeStruct((t, GATE_W), F32), jax.ShapeDtypeStruct((t, N_BRANCH * d), BF16)]
    out_specs = [row(512), row(LANES), pl.BlockSpec((None, 2 * LANES, TILE), lambda i: (i // nt, 0, i % nt))]
    out_specs += [row(w) for w in widths] + [row(GATE_W), row(N_BRANCH * d)]
    stream_in, stream_specs = (xs,), [row(d)]
    if has_moe:
        stream_in, stream_specs = (xs, moe_prev, mods_prev), [row(d), row(d), mod_spec]
        out_shape = [jax.ShapeDtypeStruct((t, d), F32)] + out_shape
        out_specs = [row(d)] + out_specs
    return pl.pallas_call(
        functools.partial(_in_kernel, has_moe=has_moe),
        grid=(t // TILE,),
        in_specs=stream_specs + [
            mod_spec,
            full(g1),
            pl.BlockSpec(w_in_p.shape, lambda i: (0, 0), pipeline_mode=pl.Buffered(1)),
            full(gmat),
            full(gqk),
            pl.BlockSpec((TILE, 256), lambda i: (i % nt, 0)),
            pl.BlockSpec((TILE, 256), lambda i: (i % nt, 0)),
            full(bcg),
            full(bmg),
        ],
        out_specs=out_specs,
        out_shape=out_shape,
        compiler_params=_cparams(("arbitrary",), 52),
    )(*stream_in, mods, g1, w_in_p, gmat, gqk, cos_t, sin_t, bcg, bmg)


def _attn_a_kernel(q_ref, k_ref, vt_ref, o_ref, s_scr):
    qt = pl.program_id(1)
    left = lax.broadcasted_iota(jnp.int32, (TILE, LANES), 1) < HEAD_DIM

    def query(hd):
        qblk = q_ref[:, (hd // 2) * LANES:(hd // 2 + 1) * LANES]
        zero = jnp.zeros_like(qblk)
        return jnp.where(left, qblk, zero) if hd % 2 == 0 else jnp.where(left, zero, qblk)

    def fold8(a):
        return functools.reduce(jnp.maximum, [a[8 * i:8 * (i + 1), :] for i in range(TILE // 8)])

    def run(n_chunks):
        rows = lambda c: slice(c * TILE, (c + 1) * TILE)

        def scores(hd, qh, c, m8):
            st = _dot_nt(k_ref[rows(c), :], qh)
            s_scr[hd % 2, rows(c), :] = st
            return jnp.maximum(m8, fold8(st))

        def values(hd, c, m, acc):
            p = jnp.exp2(s_scr[hd % 2, rows(c), :] - m).astype(BF16)
            kv = hd % 2
            return acc + _dot(vt_ref[kv * LANES:(kv + 1) * LANES, rows(c)], p)

        neg = jnp.full((8, TILE), NEG_INF, F32)
        qh = query(0)
        m8 = neg
        for c in range(n_chunks):
            m8 = scores(0, qh, c, m8)
        for hd in range(A_HEADS):
            m = jnp.max(m8, axis=0, keepdims=True)
            acc = jnp.zeros((LANES, TILE), F32)
            m8 = neg
            qh = query(hd + 1) if hd + 1 < A_HEADS else None
            for c in range(n_chunks):
                acc = values(hd, c, m, acc)
                if qh is not None:
                    m8 = scores(hd + 1, qh, c, m8)
            o_ref[hd * HEAD_DIM:(hd + 1) * HEAD_DIM, :] = (
                acc[0:HEAD_DIM, :] / acc[HEAD_DIM:HEAD_DIM + 1, :]).astype(BF16)

    @pl.when(qt == 0)
    def _():
        run(1)

    @pl.when(qt > 0)
    def _():
        run(k_ref.shape[0] // TILE)


def _attn_a(qa, ka, vat, nb, nt):
    s = nt * TILE
    w = A_HEADS * HEAD_DIM
    return pl.pallas_call(
        _attn_a_kernel,
        grid=(nb, nt),
        in_specs=[
            pl.BlockSpec((TILE, w), lambda b, i: (b * nt + i, 0)),
            pl.BlockSpec((None, s, LANES), lambda b, i: (b, 0, 0)),
            pl.BlockSpec((None, 2 * LANES, s), lambda b, i: (b, 0, 0)),
        ],
        out_specs=pl.BlockSpec((None, w, TILE), lambda b, i: (b, 0, i)),
        out_shape=jax.ShapeDtypeStruct((nb, w, s), BF16),
        scratch_shapes=[pltpu.VMEM((2, s, TILE), F32)],
        compiler_params=_cparams(("arbitrary",) * 2, 48),
    )(qa, ka.reshape(nb, s, LANES), vat)


def _attn_b_kernel(q_ref, k0_ref, k1_ref, k2_ref, kc_ref, v0_ref, v1_ref, v2_ref, vc_ref, tab_ref, o_ref):
    left = lax.broadcasted_iota(jnp.int32, (TILE, LANES), 1) < HEAD_DIM
    k_refs = (k0_ref, k1_ref, k2_ref, kc_ref)
    v_refs = (v0_ref, v1_ref, v2_ref, vc_ref)
    for p in range(NA_HEADS // 2):
        sl = slice(p * LANES, (p + 1) * LANES)
        qblk = q_ref[:, sl]
        ks = [r[:, sl] for r in k_refs]
        vs = [r[:, sl] for r in v_refs]
        outs = []
        for half in range(2):
            hd = 2 * p + half
            qh = jnp.where(left if half == 0 else jnp.logical_not(left), qblk, jnp.zeros_like(qblk))
            ss = [_dot_nt(qh, kk) for kk in ks]
            for j in range(NA_WIN_TILES):
                ss[j] = ss[j] + tab_ref[hd, :, j * TILE:(j + 1) * TILE]
            m = functools.reduce(jnp.maximum, [jnp.max(s, axis=-1, keepdims=True) for s in ss])
            ps = [jnp.exp(s - m) for s in ss]
            l = functools.reduce(jnp.add, [jnp.sum(pp, axis=-1, keepdims=True) for pp in ps])
            o = functools.reduce(jnp.add, [_dot(pp.astype(BF16), vv) for pp, vv in zip(ps, vs)])
            outs.append(o / l)
        o_ref[:, sl] = jnp.where(left, outs[0], outs[1]).astype(BF16)


def _attn_b(qb, kb, vb, table, nb, nt):
    t = qb.shape[0]
    w = NA_HEADS * HEAD_DIM
    n_lat = nt - 1

    def win(j):
        return lambda i, b: (b * nt + 1 + jnp.clip(i - 2, 0, n_lat - NA_WIN_TILES) + j, 0)

    ctx = lambda i, b: (b * nt, 0)
    variant = lambda i, b: (jnp.where(i == 0, 3, jnp.where(i == 1, 0, jnp.where(i == n_lat, 2, 1))), 0, 0, 0)
    blk = lambda f: pl.BlockSpec((TILE, w), f)
    return pl.pallas_call(
        _attn_b_kernel,
        grid=(nt, nb),
        in_specs=[blk(lambda i, b: (b * nt + i, 0)),
                  blk(win(0)), blk(win(1)), blk(win(2)), blk(ctx),
                  blk(win(0)), blk(win(1)), blk(win(2)), blk(ctx),
                  pl.BlockSpec((None, NA_HEADS, TILE, NA_WIN_TILES * TILE), variant)],
        out_specs=blk(lambda i, b: (b * nt + i, 0)),
        out_shape=jax.ShapeDtypeStruct((t, w), BF16),
        compiler_params=_cparams(("arbitrary", "arbitrary"), 48),
    )(qb, kb, kb, kb, kb, vb, vb, vb, vb, table)


def _na_table(rpb, rows):
    rq = TILE // GRID_W
    n_lat = rows // rq
    wrows = NA_WIN_TILES * rq
    win_r = min(NA_WIN_R, rows)
    col = np.arange(GRID_W)
    col_start = np.clip(col - NA_WIN_C // 2, 0, GRID_W - NA_WIN_C)
    col_ok = (col[None, :] >= col_start[:, None]) & (col[None, :] < col_start[:, None] + NA_WIN_C)
    assert np.all(np.abs(col[None, :] - col[:, None])[col_ok] < NA_WIN_C)

    nh, n_dr, n_dc = rpb.shape
    lpad = GRID_W - NA_WIN_C
    v = jnp.pad(rpb.astype(F32), ((0, 0), (0, 0), (lpad, 2 * GRID_W - lpad - n_dc)))
    skew = jnp.tile(v, (1, 1, GRID_W))[:, :, :GRID_W * (2 * GRID_W - 1)].reshape(nh, n_dr, GRID_W, 2 * GRID_W - 1)
    toep = skew[:, :, :, GRID_W - 1:]

    tabs = []
    for tq in (0, 1, n_lat - 1):
        r = tq * rq + np.arange(rq)
        kr = np.clip(tq - 1, 0, n_lat - NA_WIN_TILES) * rq + np.arange(wrows)
        rs = np.clip(r - win_r // 2, 0, rows - win_r)
        row_ok = (kr[None, :] >= rs[:, None]) & (kr[None, :] < rs[:, None] + win_r)
        dr = np.clip(kr[None, :] - r[:, None] + NA_WIN_R - 1, 0, 2 * NA_WIN_R - 2)
        ok = (row_ok[:, None, :, None] & col_ok[None, :, None, :]).reshape(TILE, wrows * GRID_W)
        blocks = jnp.stack([toep[:, a] for a in dr.ravel()], axis=1)
        bias = blocks.reshape(nh, rq, wrows, GRID_W, GRID_W).transpose(0, 1, 3, 2, 4)
        tabs.append(jnp.where(ok[None], bias.reshape(nh, TILE, wrows * GRID_W), NEG_INF))
    tabs.append(jnp.full_like(tabs[0], NEG_INF))
    return jnp.stack(tabs)


def _log_sigmoid(x):
    return jnp.minimum(x, 0.0) - jnp.log(1.0 + jnp.exp(-jnp.abs(x)))


def _mlstm_kernel(q_ref, k_ref, v_ref, g_ref, o_ref, c_s, n_s, m_s):
    direction = pl.program_id(1)
    step = pl.program_id(2)

    @pl.when(step == 0)
    def _():
        c_s[...] = jnp.zeros_like(c_s)
        n_s[...] = jnp.zeros_like(n_s)
        m_s[...] = jnp.zeros_like(m_s)

    n = TILE
    sgn = 1 - 2 * direction
    row = lax.broadcasted_iota(jnp.int32, (n, n), 0)
    col = lax.broadcasted_iota(jnp.int32, (n, n), 1)
    seen = (row - col) * sgn >= 0
    seen_bf = jnp.where(seen, 1.0, 0.0).astype(BF16)
    seen_t_bf = jnp.where((col - row) * sgn >= 0, 1.0, 0.0).astype(BF16)

    gates = g_ref[...]
    logf = _log_sigmoid(gates)
    gates_t = gates.T
    lf_hi, lf_lo = _split(logf)
    cum_c = _dot(seen_bf, lf_hi) + _dot(seen_bf, lf_lo)
    lft_hi, lft_lo = _split(logf.T)
    cum_r = _dot(lft_hi, seen_t_bf) + _dot(lft_lo, seen_t_bf)
    tot = jnp.sum(logf, axis=0, keepdims=True)

    def pick(a0, a1):
        return jnp.where(direction == 0, a0, a1)

    for hd in range(ML_HEADS):
        ci, cf = hd, 4 + hd
        ig_c = pick(gates[:, ci:ci + 1], gates[:, 8 + ci:9 + ci])
        b_c = pick(cum_c[:, cf:cf + 1], cum_c[:, 8 + cf:9 + cf])
        ig_r = pick(gates_t[ci:ci + 1, :], gates_t[8 + ci:9 + ci, :])
        b_r = pick(cum_r[cf:cf + 1, :], cum_r[8 + cf:9 + cf, :])
        b_end = pick(tot[:, cf:cf + 1], tot[:, 8 + cf:9 + cf])
        m_prev = m_s[hd, 0:1, 0:1]
        n_prev = n_s[hd, 0:1, :]
        c_prev = c_s[hd]
        sl = slice(hd * ML_HEAD_DIM, (hd + 1) * ML_HEAD_DIM)
        q = q_ref[:, sl]
        k = k_ref[:, sl]
        v = v_ref[:, sl]

        a = b_c + m_prev
        dmat = jnp.where(seen, b_c - b_r + ig_r, NEG_INF)
        m_t = jnp.maximum(a, jnp.max(dmat, axis=-1, keepdims=True))
        w_inter = jnp.exp(a - m_t)
        smat = _dot_nt(q, k) * jnp.exp(dmat - m_t)
        num = w_inter * _dot(q, c_prev.astype(BF16)) + _dot(smat.astype(BF16), v)
        qn = jnp.sum(q.astype(F32) * n_prev, axis=-1, keepdims=True)
        den = w_inter * qn + jnp.sum(smat, axis=-1, keepdims=True)
        o_ref[:, sl] = (num / jnp.maximum(jnp.abs(den), jnp.exp(-m_t))).astype(BF16)

        log_w = b_end - b_c + ig_c
        m_new = jnp.maximum(b_end + m_prev, jnp.max(log_w, axis=0, keepdims=True))
        g_inter = jnp.exp(b_end + m_prev - m_new)
        kg = k.astype(F32) * jnp.exp(log_w - m_new)
        c_s[hd] = g_inter * c_prev + _dot_tn(kg.astype(BF16), v)
        n_s[hd] = jnp.broadcast_to(g_inter * n_prev + jnp.sum(kg, axis=0, keepdims=True), n_s.shape[1:])
        m_s[hd] = jnp.broadcast_to(m_new, m_s.shape[1:])


def _mlstm(cq, ck, cv, cg, nb, nt):
    t = cq.shape[0]
    w = ML_HEADS * ML_HEAD_DIM

    def tile(b, d, s):
        return b * nt + jnp.where(d == 0, s, jnp.where(s == 0, 0, nt - s))

    blk = lambda width: pl.BlockSpec((TILE, width), lambda b, d, s: (tile(b, d, s), 0))
    return pl.pallas_call(
        _mlstm_kernel,
        grid=(nb, 2, nt),
        in_specs=[blk(w), blk(w), blk(w), blk(GATE_W)],
        out_specs=pl.BlockSpec((None, TILE, w), lambda b, d, s: (d, tile(b, d, s), 0)),
        out_shape=jax.ShapeDtypeStruct((2, t, w), BF16),
        scratch_shapes=[pltpu.VMEM((ML_HEADS, ML_HEAD_DIM, ML_HEAD_DIM), F32),
                        pltpu.VMEM((ML_HEADS, 8, ML_HEAD_DIM), F32),
                        pltpu.VMEM((ML_HEADS, 8, LANES), F32)],
        compiler_params=_cparams(("arbitrary",) * 3, 32),
    )(cq, ck, cv, cg)


def _merge_kernel(x_ref, ya_ref, yb_ref, hf_ref, hb_ref, co_ref, mg_ref, mod_ref, g2_ref, gml_ref,
                  wb_ref, wout_ref, wrh_ref, wrl_ref, brt_ref, xo_ref, h2_ref, dw_ref):
    d = D_MODEL
    mod = mod_ref[...]
    hs = hf_ref[...].astype(F32) + hb_ref[...].astype(F32)
    parts = []
    for hd in range(ML_HEADS):
        v = hs[:, hd * ML_HEAD_DIM:(hd + 1) * ML_HEAD_DIM]
        parts.append(v * lax.rsqrt(jnp.mean(v * v, axis=-1, keepdims=True) + EPS))
    ym = jnp.concatenate(parts, axis=-1) * gml_ref[...] * _sigmoid(co_ref[...].astype(F32))
    merged = mg_ref[:, 0:d].astype(F32) * _dot_tn(ya_ref[...], wb_ref[0])
    merged = merged + mg_ref[:, d:2 * d].astype(F32) * _dot(yb_ref[...], wb_ref[1])
    merged = merged + mg_ref[:, 2 * d:3 * d].astype(F32) * _dot(ym.astype(BF16), wb_ref[2])
    xn = x_ref[...] + mod[:, 2 * d:3 * d] * _dot(merged.astype(BF16), wout_ref[...])
    xo_ref[...] = xn
    h2 = _rms_mod(xn, g2_ref[...], mod[:, 3 * d:4 * d], mod[:, 4 * d:5 * d])
    h2_ref[...] = h2.astype(BF16)

    h_hi, h_lo = _split(h2)
    logits = _dot(h_hi, wrh_ref[...]) + _dot(h_hi, wrl_ref[...]) + _dot(h_lo, wrh_ref[...]) + brt_ref[...]
    lane = lax.broadcasted_iota(jnp.int32, logits.shape, 1).astype(F32)
    big = 1e9
    is_grp = (lane >= N_EXPERTS) & (lane < N_EXPERTS + N_GROUPS)
    gl = jnp.where(is_grp, logits, NEG_INF)
    gmax = jnp.max(gl, axis=-1, keepdims=True)
    gsel = jnp.min(jnp.where(gl == gmax, lane, big), axis=-1, keepdims=True) - N_EXPERTS
    p_grp = 1.0 / jnp.sum(jnp.exp(gl - gmax), axis=-1, keepdims=True)
    first = gsel * EXPERTS_PER_GROUP
    el = jnp.where((lane >= first) & (lane < first + EXPERTS_PER_GROUP), logits, NEG_INF)
    e1 = jnp.max(el, axis=-1, keepdims=True)
    i1 = jnp.min(jnp.where(el == e1, lane, big), axis=-1, keepdims=True)
    el2 = jnp.where(lane == i1, NEG_INF, el)
    e2 = jnp.max(el2, axis=-1, keepdims=True)
    i2 = jnp.min(jnp.where(el2 == e2, lane, big), axis=-1, keepdims=True)
    r = jnp.exp(e2 - e1)
    w1 = p_grp / (1.0 + r)
    dw_ref[...] = (jnp.where(lane == i1, w1, 0.0) + jnp.where(lane == i2, w1 * r, 0.0)
                   + jnp.where(lane == gsel + N_EXPERTS, 1.0, 0.0))


def _merge(xs, ya, yb, hfb, co, mg, mods, g2, gml, wb, wout, wrh, wrl, brt, nb, nt):
    t = xs.shape[0]
    d = D_MODEL
    row = lambda w: pl.BlockSpec((TILE, w), lambda i: (i, 0))
    full = lambda a: pl.BlockSpec(a.shape, lambda i: (0,) * a.ndim)
    return pl.pallas_call(
        _merge_kernel,
        grid=(t // TILE,),
        in_specs=[row(d), pl.BlockSpec((None, 512, TILE), lambda i: (i // nt, 0, i % nt)), row(512),
                  pl.BlockSpec((None, TILE, 512), lambda i: (0, i, 0)),
                  pl.BlockSpec((None, TILE, 512), lambda i: (1, i, 0)),
                  row(512), row(N_BRANCH * d),
                  pl.BlockSpec((None, 1, 6 * d), _tile_mod_index(nt, nb)),
                  full(g2), full(gml), full(wb), full(wout), full(wrh), full(wrl), full(brt)],
        out_specs=[row(d), row(d), row(LANES)],
        out_shape=[jax.ShapeDtypeStruct((t, d), F32), jax.ShapeDtypeStruct((t, d), BF16),
                   jax.ShapeDtypeStruct((t, LANES), F32)],
        compiler_params=_cparams(("arbitrary",), 48),
    )(xs, ya, yb, hfb, hfb, co, mg, mods, g2, gml, wb, wout, wrh, wrl, brt)


MOE_TILE = 1024
MOE_CHUNK = 128


def _moe_kernel(seg_ref, h_ref, dw_ref, wgu_ref, wd_ref, o_ref, hs_s, dws_s, acc_s, pt_s):
    i = pl.program_id(0)
    g = pl.program_id(1)
    tm = h_ref.shape[0]
    ng = N_GROUPS

    @pl.when(g == 0)
    def _():
        dw = dw_ref[...]
        lane = lax.broadcasted_iota(jnp.int32, dw.shape, 1)
        onehot = jnp.where((lane >= N_EXPERTS) & (lane < N_EXPERTS + ng), dw, 0.0)
        onehot_t = onehot.T
        row = lax.broadcasted_iota(jnp.int32, (tm, tm), 0)
        col = lax.broadcasted_iota(jnp.int32, (tm, tm), 1)
        rank_c = _dot(jnp.where(col < row, 1.0, 0.0).astype(BF16), onehot.astype(BF16))
        rank_r = _dot(onehot_t.astype(BF16), jnp.where(row < col, 1.0, 0.0).astype(BF16))
        sub = lax.broadcasted_iota(jnp.int32, onehot_t.shape, 0)
        lo_c = jnp.zeros(dw.shape, F32)
        lo_r = jnp.zeros(onehot_t.shape, F32)
        for k in range(ng):
            lo = seg_ref[(i * ng + k) * 2].astype(F32)
            lo_c = jnp.where(lane == N_EXPERTS + k, lo, lo_c)
            lo_r = jnp.where(sub == N_EXPERTS + k, lo, lo_r)
        pos_c = jnp.sum(onehot * (rank_c + lo_c), axis=1, keepdims=True)
        pos_r = jnp.sum(onehot_t * (rank_r + lo_r), axis=0, keepdims=True)
        perm = jnp.where(pos_r == row.astype(F32), 1.0, 0.0).astype(BF16)
        pt_s[...] = jnp.where(pos_c == col.astype(F32), 1.0, 0.0).astype(BF16)
        hs_s[...] = _dot(perm, h_ref[...]).astype(BF16)
        d1 = dw.astype(BF16)
        r1 = dw - d1.astype(F32)
        d2 = r1.astype(BF16)
        d3 = (r1 - d2.astype(F32)).astype(BF16)
        dws_s[...] = _dot(perm, d1) + _dot(perm, d2) + _dot(perm, d3)
        acc_s[...] = jnp.zeros_like(acc_s)

    lo = seg_ref[(i * ng + g) * 2]
    hi = seg_ref[(i * ng + g) * 2 + 1]
    c_lo = lo // MOE_CHUNK
    c_hi = jnp.where(hi > lo, (hi + MOE_CHUNK - 1) // MOE_CHUNK, c_lo)
    half = EXPERTS_PER_GROUP * D_EXPERT

    def chunk(c, carry):
        rows = pl.ds(pl.multiple_of(c * MOE_CHUNK, MOE_CHUNK), MOE_CHUNK)
        gu = _dot(hs_s[rows, :], wgu_ref[...])
        gate = gu[:, :half]
        act = gate * _sigmoid(gate) * gu[:, half:]
        dwc = dws_s[rows, :]
        lane = lax.broadcasted_iota(jnp.int32, dwc.shape, 1)
        scale = jnp.concatenate(
            [jnp.broadcast_to(jnp.sum(jnp.where(lane == g * EXPERTS_PER_GROUP + k, dwc, 0.0), axis=1, keepdims=True),
                              (MOE_CHUNK, D_EXPERT)) for k in range(EXPERTS_PER_GROUP)], axis=1)
        acc_s[rows, :] += _dot((act * scale).astype(BF16), wd_ref[...])
        return carry

    lax.fori_loop(c_lo, c_hi, chunk, 0)

    @pl.when(g == ng - 1)
    def _():
        o_ref[...] = _dot(pt_s[...], acc_s[...].astype(BF16)).astype(BF16)


def _moe(h2, dw, wgu, wd):
    t, d = h2.shape
    tm = MOE_TILE
    n_tiles = t // tm
    cnt = jnp.sum(dw[:, N_EXPERTS:N_EXPERTS + N_GROUPS].reshape(n_tiles, tm, N_GROUPS), axis=1).astype(jnp.int32)
    hi = jnp.cumsum(cnt, axis=1)
    seg = jnp.stack([hi - cnt, hi], axis=-1).reshape(-1)
    half = EXPERTS_PER_GROUP * D_EXPERT
    grid_spec = pltpu.PrefetchScalarGridSpec(
        num_scalar_prefetch=1,
        grid=(n_tiles, N_GROUPS),
        in_specs=[pl.BlockSpec((tm, d), lambda i, g, seg: (i, 0)),
                  pl.BlockSpec((tm, LANES), lambda i, g, seg: (i, 0)),
                  pl.BlockSpec((None, d, 2 * half), lambda i, g, seg: (g, 0, 0)),
                  pl.BlockSpec((None, half, d), lambda i, g, seg: (g, 0, 0))],
        out_specs=pl.BlockSpec((tm, d), lambda i, g, seg: (i, 0)),
        scratch_shapes=[pltpu.VMEM((tm, d), BF16), pltpu.VMEM((tm, LANES), F32),
                        pltpu.VMEM((tm, d), F32), pltpu.VMEM((tm, tm), BF16)],
    )
    return pl.pallas_call(
        _moe_kernel,
        grid_spec=grid_spec,
        out_shape=jax.ShapeDtypeStruct((t, d), BF16),
        compiler_params=_cparams(("arbitrary", "arbitrary"), 56),
    )(seg, h2, dw, wgu, wd)


def _group_weights(w_gate_up, w_down):
    d = w_gate_up.shape[1]
    wgu = w_gate_up.reshape(N_GROUPS, EXPERTS_PER_GROUP, d, 2, D_EXPERT).transpose(0, 2, 3, 1, 4)
    wgu = wgu.reshape(N_GROUPS, d, 2 * EXPERTS_PER_GROUP * D_EXPERT)
    wd = w_down.reshape(N_GROUPS, EXPERTS_PER_GROUP * D_EXPERT, d)
    return wgu.astype(BF16), wd.astype(BF16)


def _final_kernel(x_ref, y_ref, mod_ref, o_ref):
    d = D_MODEL
    o_ref[...] = x_ref[...] + mod_ref[:, 5 * d:6 * d] * y_ref[...].astype(F32)


def _final(xn, y, mods, nb, nt):
    d = D_MODEL
    blk = pl.BlockSpec((TILE, d), lambda b, j: (b * nt + 1 + j, 0))
    return pl.pallas_call(
        _final_kernel,
        grid=(nb, nt - 1),
        in_specs=[blk, blk, pl.BlockSpec((None, 1, 6 * d), lambda b, j: (b, 0, 0))],
        out_specs=pl.BlockSpec((None, TILE, d), lambda b, j: (b, j, 0)),
        out_shape=jax.ShapeDtypeStruct((nb, (nt - 1) * TILE, d), F32),
        compiler_params=_cparams(("arbitrary", "arbitrary"), 32),
    )(xn, y, mods)


def _rope_tables(n_tok, n_ctx):
    t = jnp.arange(n_tok)
    rowp = (t // GRID_W).astype(F32)
    colp = (t % GRID_W).astype(F32)
    n_freq = HEAD_DIM // 4
    inv = ROPE_THETA ** (-jnp.arange(n_freq, dtype=F32) / n_freq)
    ang = jnp.concatenate([rowp[:, None] * inv, colp[:, None] * inv], axis=-1)
    cos, sin = jnp.cos(ang), jnp.sin(ang)
    cos64 = jnp.concatenate([cos, cos], axis=-1)
    sin64 = jnp.concatenate([-sin, sin], axis=-1)
    pad = lambda a, v: jnp.concatenate([jnp.full((n_ctx, HEAD_DIM), v, F32), a], axis=0)
    return jnp.tile(pad(cos64, 1.0), (1, 4)), jnp.tile(pad(sin64, 0.0), (1, 4))


def kernel(x, c, ctx, c_ctx, w_mod, b_mod, g_norm, w_in, b_merge, g_qk, rpb, b_mlstm, g_ml, w_branch, w_out,
           w_group, b_group, w_router, b_router, w_gate_up, w_down):
    nb, n_tok, d = x.shape
    n_ctx = ctx.shape[1]
    depth = w_mod.shape[0]
    assert d == D_MODEL and n_ctx == TILE and n_tok % TILE == 0
    rows = n_tok // GRID_W
    nt = (n_ctx + n_tok) // TILE
    assert nt - 1 >= NA_WIN_TILES and rows >= NA_WIN_R
    t = nb * nt * TILE

    xs = jnp.concatenate([ctx, x], axis=1).reshape(t, d)
    n_mod = -(-(nb + 1) // 8) * 8
    cc = jnp.zeros((n_mod, d), F32).at[:nb].set(c).at[nb].set(c_ctx)
    mods_all = _modulation(cc, w_mod, b_mod)

    cos_t, sin_t = _rope_tables(n_tok, n_ctx)
    gmat = jnp.asarray(np.kron(np.eye(256 // HEAD_DIM), np.ones((HEAD_DIM, HEAD_DIM))), BF16)
    a_cols = np.concatenate([np.arange(h * HEAD_DIM, (h + 1) * HEAD_DIM) for h in _A_HEAD_ORDER])

    assert t % MOE_TILE == 0
    moe_y = mods_prev = None
    for l in range(depth):
        mods = mods_all[l].reshape(n_mod, 1, 6 * d)
        wl = w_in[l]
        w_in_p = jnp.concatenate(
            [wl[:, a_cols], wl[:, 512:4352], jnp.pad(wl[:, 4352:4368], ((0, 0), (0, GATE_W - 16))), wl[:, 4368:]],
            axis=1).astype(BF16)
        assert w_in_p.shape[1] == IN_WIDTH_P
        gqk = jnp.pad(jnp.tile(g_qk[l], (1, 256 // HEAD_DIM)), ((0, 4), (0, 0)))
        bcg = jnp.pad(b_mlstm[l].reshape(1, 16), ((0, 0), (0, GATE_W - 16)))
        bmg = b_merge[l].reshape(1, N_BRANCH * d)
        outs = _in_projection(xs, moe_y, mods_prev, mods, g_norm[l, 0:1], w_in_p, gmat, gqk, cos_t, sin_t,
                              bcg, bmg, nb, nt)
        if moe_y is not None:
            xs, *outs = outs
        qa, ka, vat, qb, kb, vb, cq, ck, cv, co, cg, mg = outs

        ya = _attn_a(qa, ka, vat, nb, nt)
        yb = _attn_b(qb, kb, vb, _na_table(rpb[l], rows), nb, nt)
        hfb = _mlstm(cq, ck, cv, cg, nb, nt)

        wb = jnp.stack([w_branch[l, 0][a_cols], w_branch[l, 1], w_branch[l, 2]]).astype(BF16)
        w_rt = jnp.concatenate([w_router[l], w_group[l], jnp.zeros((d, LANES - N_EXPERTS - N_GROUPS), F32)], axis=1)
        wrh = w_rt.astype(BF16)
        wrl = (w_rt - wrh.astype(F32)).astype(BF16)
        brt = jnp.concatenate([b_router[l], b_group[l], jnp.zeros((LANES - N_EXPERTS - N_GROUPS,), F32)])[None, :]
        xs, h2, dw = _merge(xs, ya, yb, hfb, co, mg, mods, g_norm[l, 1:2], g_ml[l][None, :], wb,
                            w_out[l].astype(BF16), wrh, wrl, brt, nb, nt)
        moe_y = _moe(h2, dw, *_group_weights(w_gate_up[l], w_down[l]))
        mods_prev = mods

    return _final(xs, moe_y, mods_prev, nb, nt)
```

```python
import functools

import numpy as np
import jax
import jax.numpy as jnp
from jax import lax
from jax.experimental import pallas as pl
from jax.experimental.pallas import tpu as pltpu

D_MODEL = 1024
GRID_W = 64
HEAD_DIM = 64
ROPE_THETA = 10000.0
A_HEADS = 8
A_KV_HEADS = 2
NA_HEADS = 8
NA_WIN_R = 8
NA_WIN_C = 16
ML_HEADS = 4
ML_HEAD_DIM = 128
N_BRANCH = 3
BRANCH_W = 512
N_GROUPS = 4
EXPERTS_PER_GROUP = 8
N_EXPERTS = N_GROUPS * EXPERTS_PER_GROUP
D_EXPERT = 256
EPS = 1e-6
NEG_INF = -1e30
LOG2E = 1.4426950408889634

TILE = 256
LANES = 128
NA_WIN_TILES = 3
GATE_W = 128
V7X_VMEM_BYTES = 64 * 1024 * 1024

F32 = jnp.float32
BF16 = jnp.bfloat16

_O_QA, _O_KA, _O_VA = 0, 512, 640
_O_QB, _O_KB, _O_VB = 768, 1280, 1792
_O_CQ, _O_CK, _O_CV, _O_CO = 2304, 2816, 3328, 3840
_O_CG = 4352
_O_MG = _O_CG + GATE_W
IN_WIDTH_P = _O_MG + N_BRANCH * D_MODEL
_A_HEAD_ORDER = (0, 4, 1, 5, 2, 6, 3, 7)


def _cparams(sem, vmem_mb):
    return pltpu.CompilerParams(dimension_semantics=sem, vmem_limit_bytes=vmem_mb * 1024 * 1024)


def _dot(a, b):
    return jnp.dot(a, b, preferred_element_type=F32)


def _dot_nt(a, b):
    return lax.dot_general(a, b, (((1,), (1,)), ((), ())), preferred_element_type=F32)


def _dot_tn(a, b):
    return lax.dot_general(a, b, (((0,), (0,)), ((), ())), preferred_element_type=F32)


def _split(a):
    hi = a.astype(BF16)
    lo = (a - hi.astype(F32)).astype(BF16)
    return hi, lo


def _sigmoid(x):
    return 1.0 / (1.0 + jnp.exp(-x))


def _rms_mod(x, g, shift, scale):
    ms = jnp.mean(x * x, axis=-1, keepdims=True)
    y = x * lax.rsqrt(ms + EPS) * g
    return y * (1.0 + scale) + shift


def _mod_kernel(c_ref, w_ref, b_ref, o_ref):
    c = c_ref[...]
    a_hi, a_lo = _split(c * _sigmoid(c))
    w_hi, w_lo = _split(w_ref[...])
    o_ref[...] = _dot(a_hi, w_hi) + _dot(a_hi, w_lo) + _dot(a_lo, w_hi) + b_ref[...]


def _modulation(cc, w_mod, b_mod):
    depth, d, n = w_mod.shape
    r = cc.shape[0]
    bn = 1024
    return pl.pallas_call(
        _mod_kernel,
        grid=(depth, n // bn),
        in_specs=[
            pl.BlockSpec((r, d), lambda l, j: (0, 0)),
            pl.BlockSpec((None, d, bn), lambda l, j: (l, 0, j)),
            pl.BlockSpec((None, 1, bn), lambda l, j: (l, 0, j)),
        ],
        out_specs=pl.BlockSpec((None, r, bn), lambda l, j: (l, 0, j)),
        out_shape=jax.ShapeDtypeStruct((depth, r, n), F32),
        compiler_params=_cparams(("arbitrary", "arbitrary"), 32),
    )(cc, w_mod, b_mod.reshape(depth, 1, n))


def _group_norm64(acc, gmat, gain):
    hi, lo = _split(acc * acc)
    ss = _dot(hi, gmat) + _dot(lo, gmat)
    return acc * lax.rsqrt(ss * (1.0 / HEAD_DIM) + EPS) * gain


def _rope(x, cos, sin_signed):
    w = x.shape[1]
    lane = lax.broadcasted_iota(jnp.int32, x.shape, 1)
    nxt = pltpu.roll(x, w - HEAD_DIM // 2, 1)
    prv = pltpu.roll(x, HEAD_DIM // 2, 1)
    rot = jnp.where((lane % HEAD_DIM) < HEAD_DIM // 2, nxt, prv)
    return x * cos + rot * sin_signed


def _in_kernel(*refs, has_moe):
    d = D_MODEL
    if has_moe:
        x_ref, y_ref, modp_ref, *refs = refs
    else:
        x_ref, *refs = refs
    mod_ref, g_ref, w_ref, gmat_ref, gqk_ref, cos_ref, sin_ref, bcg_ref, bmg_ref, *outs = refs
    x = x_ref[...]
    if has_moe:
        xo_ref, *outs = outs
        x = x + modp_ref[:, 5 * d:6 * d] * y_ref[...].astype(F32)
        xo_ref[...] = x
    qa_ref, ka_ref, vat_ref, qb_ref, kb_ref, vbt_ref, cq_ref, ck_ref, cv_ref, co_ref, cg_ref, mg_ref = outs
    mod = mod_ref[...]
    h = _rms_mod(x, g_ref[...], mod[:, 0:d], mod[:, d:2 * d]).astype(BF16)

    def proj(a, width):
        return _dot(h, w_ref[:, a:a + width])

    gmat = gmat_ref[...]
    cos = cos_ref[...]
    sin = sin_ref[...]
    att_scale = HEAD_DIM ** -0.5
    for s in range(2):
        acc = _group_norm64(proj(_O_QA + 256 * s, 256), gmat, gqk_ref[0:1, :])
        qa_ref[:, 256 * s:256 * (s + 1)] = (_rope(acc, cos, sin) * (att_scale * LOG2E)).astype(BF16)
    acc = _group_norm64(proj(_O_KA, 128), gmat[:128, :128], gqk_ref[1:2, :128])
    ka_ref[...] = _rope(acc, cos[:, :128], sin[:, :128]).astype(BF16)
    vt = proj(_O_VA, 128).T
    ones = jnp.ones((HEAD_DIM, TILE), F32)
    vat_ref[...] = jnp.concatenate([vt[:HEAD_DIM], ones, vt[HEAD_DIM:], ones], axis=0).astype(BF16)
    for s in range(2):
        acc = _group_norm64(proj(_O_QB + 256 * s, 256), gmat, gqk_ref[2:3, :])
        qb_ref[:, 256 * s:256 * (s + 1)] = (acc * (att_scale * LOG2E)).astype(BF16)
        acc = _group_norm64(proj(_O_KB + 256 * s, 256), gmat, gqk_ref[3:4, :])
        kb_ref[:, 256 * s:256 * (s + 1)] = acc.astype(BF16)
    vbt = proj(_O_VB, 512).T
    ones = jnp.ones((NA_V_ROWS - HEAD_DIM, TILE), F32)
    vbt_ref[...] = jnp.concatenate(
        [piece for hd in range(NA_HEADS) for piece in (vbt[hd * HEAD_DIM:(hd + 1) * HEAD_DIM], ones)],
        axis=0).astype(BF16)
    cq_ref[...] = proj(_O_CQ, 512).astype(BF16)
    ck_ref[...] = (proj(_O_CK, 512) * (ML_HEAD_DIM ** -0.5)).astype(BF16)
    cv_ref[...] = proj(_O_CV, 512).astype(BF16)
    co_ref[...] = proj(_O_CO, 512).astype(BF16)
    cg_ref[...] = proj(_O_CG, GATE_W) + bcg_ref[...]
    for s in range(N_BRANCH * D_MODEL // 512):
        a = 512 * s
        mg_ref[:, a:a + 512] = _sigmoid(proj(_O_MG + a, 512) + bmg_ref[:, a:a + 512]).astype(BF16)


def _tile_mod_index(nt, nb):
    return lambda i: (jnp.where(i % nt == 0, nb, i // nt), 0, 0)


def _in_projection(xs, moe_prev, mods_prev, mods, g1, w_in_p, gmat, gqk, cos_t, sin_t, bcg, bmg, nb, nt):
    t = xs.shape[0]
    d = D_MODEL
    has_moe = moe_prev is not None
    row = lambda w: pl.BlockSpec((TILE, w), lambda i: (i, 0))
    full = lambda a: pl.BlockSpec(a.shape, lambda i: (0,) * a.ndim)
    mod_spec = pl.BlockSpec((None, 1, 6 * d), _tile_mod_index(nt, nb))
    act = jax.ShapeDtypeStruct((t, 512), BF16)
    vbt_rows = NA_HEADS * NA_V_ROWS
    out_shape = [act, jax.ShapeDtypeStruct((t, LANES), BF16), jax.ShapeDtypeStruct((nb, 2 * LANES, nt * TILE), BF16),
                 act, act, jax.ShapeDtypeStruct((t // TILE, vbt_rows, TILE), BF16), act, act, act, act,
                 jax.ShapeDtypeStruct((t, GATE_W), F32), jax.ShapeDtypeStruct((t, N_BRANCH * d), BF16)]
    out_specs = [row(512), row(LANES), pl.BlockSpec((None, 2 * LANES, TILE), lambda i: (i // nt, 0, i % nt)),
                 row(512), row(512), pl.BlockSpec((None, vbt_rows, TILE), lambda i: (i, 0, 0)),
                 row(512), row(512), row(512), row(512), row(GATE_W), row(N_BRANCH * d)]
    stream_in, stream_specs = (xs,), [row(d)]
    if has_moe:
        stream_in, stream_specs = (xs, moe_prev, mods_prev), [row(d), row(d), mod_spec]
        out_shape = [jax.ShapeDtypeStruct((t, d), F32)] + out_shape
        out_specs = [row(d)] + out_specs
    return pl.pallas_call(
        functools.partial(_in_kernel, has_moe=has_moe),
        grid=(t // TILE,),
        in_specs=stream_specs + [
            mod_spec,
            full(g1),
            pl.BlockSpec(w_in_p.shape, lambda i: (0, 0), pipeline_mode=pl.Buffered(1)),
            full(gmat),
            full(gqk),
            pl.BlockSpec((TILE, 256), lambda i: (i % nt, 0)),
            pl.BlockSpec((TILE, 256), lambda i: (i % nt, 0)),
            full(bcg),
            full(bmg),
        ],
        out_specs=out_specs,
        out_shape=out_shape,
        compiler_params=_cparams(("arbitrary",), 52),
    )(*stream_in, mods, g1, w_in_p, gmat, gqk, cos_t, sin_t, bcg, bmg)


def _attn_a_kernel(q_ref, k_ref, vt_ref, o_ref, s_scr):
    qt = pl.program_id(1)
    left = lax.broadcasted_iota(jnp.int32, (TILE, LANES), 1) < HEAD_DIM

    def query(hd):
        qblk = q_ref[:, (hd // 2) * LANES:(hd // 2 + 1) * LANES]
        zero = jnp.zeros_like(qblk)
        return jnp.where(left, qblk, zero) if hd % 2 == 0 else jnp.where(left, zero, qblk)

    def fold8(a):
        return functools.reduce(jnp.maximum, [a[8 * i:8 * (i + 1), :] for i in range(TILE // 8)])

    def run(n_chunks):
        rows = lambda c: slice(c * TILE, (c + 1) * TILE)

        def scores(hd, qh, c, m8):
            st = _dot_nt(k_ref[rows(c), :], qh)
            s_scr[hd % 2, rows(c), :] = st
            return jnp.maximum(m8, fold8(st))

        def values(hd, c, m, acc):
            p = jnp.exp2(s_scr[hd % 2, rows(c), :] - m).astype(BF16)
            kv = hd % 2
            return acc + _dot(vt_ref[kv * LANES:(kv + 1) * LANES, rows(c)], p)

        neg = jnp.full((8, TILE), NEG_INF, F32)
        qh = query(0)
        m8 = neg
        for c in range(n_chunks):
            m8 = scores(0, qh, c, m8)
        for hd in range(A_HEADS):
            m = jnp.max(m8, axis=0, keepdims=True)
            acc = jnp.zeros((LANES, TILE), F32)
            m8 = neg
            qh = query(hd + 1) if hd + 1 < A_HEADS else None
            for c in range(n_chunks):
                acc = values(hd, c, m, acc)
                if qh is not None:
                    m8 = scores(hd + 1, qh, c, m8)
            o_ref[hd * HEAD_DIM:(hd + 1) * HEAD_DIM, :] = (
                acc[0:HEAD_DIM, :] / acc[HEAD_DIM:HEAD_DIM + 1, :]).astype(BF16)

    @pl.when(qt == 0)
    def _():
        run(1)

    @pl.when(qt > 0)
    def _():
        run(k_ref.shape[0] // TILE)


def _attn_a(qa, ka, vat, nb, nt):
    s = nt * TILE
    w = A_HEADS * HEAD_DIM
    return pl.pallas_call(
        _attn_a_kernel,
        grid=(nb, nt),
        in_specs=[
            pl.BlockSpec((TILE, w), lambda b, i: (b * nt + i, 0)),
            pl.BlockSpec((None, s, LANES), lambda b, i: (b, 0, 0)),
            pl.BlockSpec((None, 2 * LANES, s), lambda b, i: (b, 0, 0)),
        ],
        out_specs=pl.BlockSpec((None, w, TILE), lambda b, i: (b, 0, i)),
        out_shape=jax.ShapeDtypeStruct((nb, w, s), BF16),
        scratch_shapes=[pltpu.VMEM((2, s, TILE), F32)],
        compiler_params=_cparams(("arbitrary",) * 2, 48),
    )(qa, ka.reshape(nb, s, LANES), vat)


NA_V_ROWS = HEAD_DIM + 16


def _attn_b_kernel(q_ref, k0_ref, k1_ref, k2_ref, kc_ref, v0_ref, v1_ref, v2_ref, vc_ref, tab_ref, o_ref, s_scr):
    left = lax.broadcasted_iota(jnp.int32, (TILE, LANES), 1) < HEAD_DIM
    k_refs = (k0_ref, k1_ref, k2_ref, kc_ref)
    v_refs = (v0_ref, v1_ref, v2_ref, vc_ref)
    n_blk = len(k_refs)

    def query(hd):
        qblk = q_ref[:, (hd // 2) * LANES:(hd // 2 + 1) * LANES]
        zero = jnp.zeros_like(qblk)
        return jnp.where(left, qblk, zero) if hd % 2 == 0 else jnp.where(left, zero, qblk)

    def fold8(a):
        return functools.reduce(jnp.maximum, [a[8 * i:8 * (i + 1), :] for i in range(TILE // 8)])

    def scores(hd, qh, j, m8):
        st = _dot_nt(k_refs[j][:, (hd // 2) * LANES:(hd // 2 + 1) * LANES], qh)
        if j < NA_WIN_TILES:
            st = st + tab_ref[hd, j * TILE:(j + 1) * TILE, :]
        s_scr[hd % 2, j] = st
        return jnp.maximum(m8, fold8(st))

    def values(hd, j, m, acc):
        p = jnp.exp2(s_scr[hd % 2, j] - m).astype(BF16)
        return acc + _dot(v_refs[j][hd * NA_V_ROWS:(hd + 1) * NA_V_ROWS, :], p)

    neg = jnp.full((8, TILE), NEG_INF, F32)
    qh = query(0)
    m8 = neg
    for j in range(n_blk):
        m8 = scores(0, qh, j, m8)
    for hd in range(NA_HEADS):
        m = jnp.max(m8, axis=0, keepdims=True)
        acc = jnp.zeros((NA_V_ROWS, TILE), F32)
        m8 = neg
        qh = query(hd + 1) if hd + 1 < NA_HEADS else None
        for j in range(n_blk):
            acc = values(hd, j, m, acc)
            if qh is not None:
                m8 = scores(hd + 1, qh, j, m8)
        o_ref[hd * HEAD_DIM:(hd + 1) * HEAD_DIM, :] = (
            acc[0:HEAD_DIM, :] / acc[HEAD_DIM:HEAD_DIM + 1, :]).astype(BF16)


def _attn_b(qb, kb, vbt, table, nb, nt):
    w = NA_HEADS * HEAD_DIM
    n_lat = nt - 1

    def win(j):
        return lambda i, b: b * nt + 1 + jnp.clip(i - 2, 0, n_lat - NA_WIN_TILES) + j

    ctx = lambda i, b: b * nt
    variant = lambda i, b: (jnp.where(i == 0, 3, jnp.where(i == 1, 0, jnp.where(i == n_lat, 2, 1))), 0, 0, 0)
    kblk = lambda f: pl.BlockSpec((TILE, w), lambda i, b: (f(i, b), 0))
    vblk = lambda f: pl.BlockSpec((None, NA_HEADS * NA_V_ROWS, TILE), lambda i, b: (f(i, b), 0, 0))
    return pl.pallas_call(
        _attn_b_kernel,
        grid=(nt, nb),
        in_specs=[kblk(lambda i, b: b * nt + i),
                  kblk(win(0)), kblk(win(1)), kblk(win(2)), kblk(ctx),
                  vblk(win(0)), vblk(win(1)), vblk(win(2)), vblk(ctx),
                  pl.BlockSpec((None, NA_HEADS, NA_WIN_TILES * TILE, TILE), variant)],
        out_specs=pl.BlockSpec((None, w, TILE), lambda i, b: (b * nt + i, 0, 0)),
        out_shape=jax.ShapeDtypeStruct((nb * nt, w, TILE), BF16),
        scratch_shapes=[pltpu.VMEM((2, NA_WIN_TILES + 1, TILE, TILE), F32)],
        compiler_params=_cparams(("arbitrary", "arbitrary"), 48),
    )(qb, kb, kb, kb, kb, vbt, vbt, vbt, vbt, table)


def _na_table(rpb, rows):
    rq = TILE // GRID_W
    n_lat = rows // rq
    wrows = NA_WIN_TILES * rq
    win_r = min(NA_WIN_R, rows)
    col = np.arange(GRID_W)
    col_start = np.clip(col - NA_WIN_C // 2, 0, GRID_W - NA_WIN_C)
    col_ok = (col[None, :] >= col_start[:, None]) & (col[None, :] < col_start[:, None] + NA_WIN_C)
    assert np.all(np.abs(col[None, :] - col[:, None])[col_ok] < NA_WIN_C)

    nh, n_dr, n_dc = rpb.shape
    lpad = GRID_W - NA_WIN_C
    v = jnp.pad(rpb.astype(F32), ((0, 0), (0, 0), (lpad, 2 * GRID_W - lpad - n_dc)))
    skew = jnp.tile(v, (1, 1, GRID_W))[:, :, :GRID_W * (2 * GRID_W - 1)].reshape(nh, n_dr, GRID_W, 2 * GRID_W - 1)
    toep = skew[:, :, :, GRID_W - 1:]

    tabs = []
    for tq in (0, 1, n_lat - 1):
        r = tq * rq + np.arange(rq)
        kr = np.clip(tq - 1, 0, n_lat - NA_WIN_TILES) * rq + np.arange(wrows)
        rs = np.clip(r - win_r // 2, 0, rows - win_r)
        row_ok = (kr[None, :] >= rs[:, None]) & (kr[None, :] < rs[:, None] + win_r)
        dr = np.clip(kr[None, :] - r[:, None] + NA_WIN_R - 1, 0, 2 * NA_WIN_R - 2)
        ok = (row_ok[:, None, :, None] & col_ok[None, :, None, :]).reshape(TILE, wrows * GRID_W)
        blocks = jnp.stack([toep[:, a] for a in dr.ravel()], axis=1)
        bias = blocks.reshape(nh, rq, wrows, GRID_W, GRID_W).transpose(0, 2, 4, 1, 3)
        tabs.append(jnp.where(ok.T[None], bias.reshape(nh, wrows * GRID_W, TILE) * LOG2E, NEG_INF))
    tabs.append(jnp.full_like(tabs[0], NEG_INF))
    return jnp.stack(tabs)


def _log_sigmoid(x):
    return jnp.minimum(x, 0.0) - jnp.log(1.0 + jnp.exp(-jnp.abs(x)))


def _mlstm_kernel(qf_ref, kf_ref, vf_ref, gf_ref, qb_ref, kb_ref, vb_ref, gb_ref, of_ref, ob_ref, c_s, n_s, m_s):
    step = pl.program_id(1)

    @pl.when(step == 0)
    def _():
        c_s[...] = jnp.zeros_like(c_s)
        n_s[...] = jnp.zeros_like(n_s)
        m_s[...] = jnp.zeros_like(m_s)

    n = TILE
    row = lax.broadcasted_iota(jnp.int32, (n, n), 0)
    col = lax.broadcasted_iota(jnp.int32, (n, n), 1)
    lower, upper = row >= col, row <= col
    dirs = ((qf_ref, kf_ref, vf_ref, gf_ref, of_ref, lower, upper), (qb_ref, kb_ref, vb_ref, gb_ref, ob_ref, upper, lower))
    for direction, (q_ref, k_ref, v_ref, g_ref, o_ref, seen, seen_t) in enumerate(dirs):
        seen_bf = jnp.where(seen, 1.0, 0.0).astype(BF16)
        seen_t_bf = jnp.where(seen_t, 1.0, 0.0).astype(BF16)
        gates = g_ref[...]
        logf = _log_sigmoid(gates)
        gates_t = gates.T
        lf_hi, lf_lo = _split(logf)
        cum_c = _dot(seen_bf, lf_hi) + _dot(seen_bf, lf_lo)
        lft_hi, lft_lo = _split(logf.T)
        cum_r = _dot(lft_hi, seen_t_bf) + _dot(lft_lo, seen_t_bf)
        tot = jnp.sum(logf, axis=0, keepdims=True)

        for hd in range(ML_HEADS):
            ci = 8 * direction + hd
            cf = ci + 4
            st = direction * ML_HEADS + hd
            ig_c = gates[:, ci:ci + 1]
            b_c = cum_c[:, cf:cf + 1]
            ig_r = gates_t[ci:ci + 1, :]
            b_r = cum_r[cf:cf + 1, :]
            b_end = tot[:, cf:cf + 1]
            m_prev = m_s[st, 0:1, 0:1]
            n_prev = n_s[st, 0:1, :]
            c_prev = c_s[st]
            sl = slice(hd * ML_HEAD_DIM, (hd + 1) * ML_HEAD_DIM)
            q = q_ref[:, sl]
            k = k_ref[:, sl]
            v = v_ref[:, sl]

            a = b_c + m_prev
            dmat = jnp.where(seen, b_c - b_r + ig_r, NEG_INF)
            m_t = jnp.maximum(a, jnp.max(dmat, axis=-1, keepdims=True))
            w_inter = jnp.exp(a - m_t)
            smat = _dot_nt(q, k) * jnp.exp(dmat - m_t)
            num = w_inter * _dot(q, c_prev.astype(BF16)) + _dot(smat.astype(BF16), v)
            qn = jnp.sum(q.astype(F32) * n_prev, axis=-1, keepdims=True)
            den = w_inter * qn + jnp.sum(smat, axis=-1, keepdims=True)
            o_ref[:, sl] = (num / jnp.maximum(jnp.abs(den), jnp.exp(-m_t))).astype(BF16)

            log_w = b_end - b_c + ig_c
            m_new = jnp.maximum(b_end + m_prev, jnp.max(log_w, axis=0, keepdims=True))
            g_inter = jnp.exp(b_end + m_prev - m_new)
            kg = k.astype(F32) * jnp.exp(log_w - m_new)
            c_s[st] = g_inter * c_prev + _dot_tn(kg.astype(BF16), v)
            n_s[st] = jnp.broadcast_to(g_inter * n_prev + jnp.sum(kg, axis=0, keepdims=True), n_s.shape[1:])
            m_s[st] = jnp.broadcast_to(m_new, m_s.shape[1:])


def _mlstm(cq, ck, cv, cg, nb, nt):
    t = cq.shape[0]
    w = ML_HEADS * ML_HEAD_DIM
    fwd = lambda b, s: (b * nt + s, 0)
    bwd = lambda b, s: (b * nt + jnp.where(s == 0, 0, nt - s), 0)
    blk = lambda width, f: pl.BlockSpec((TILE, width), f)
    n_state = 2 * ML_HEADS
    return pl.pallas_call(
        _mlstm_kernel,
        grid=(nb, nt),
        in_specs=[blk(w, fwd), blk(w, fwd), blk(w, fwd), blk(GATE_W, fwd),
                  blk(w, bwd), blk(w, bwd), blk(w, bwd), blk(GATE_W, bwd)],
        out_specs=[blk(w, fwd), blk(w, bwd)],
        out_shape=[jax.ShapeDtypeStruct((t, w), BF16)] * 2,
        scratch_shapes=[pltpu.VMEM((n_state, ML_HEAD_DIM, ML_HEAD_DIM), F32),
                        pltpu.VMEM((n_state, 8, ML_HEAD_DIM), F32),
                        pltpu.VMEM((n_state, 8, LANES), F32)],
        compiler_params=_cparams(("arbitrary",) * 2, 32),
    )(cq, ck, cv, cg, cq, ck, cv, cg)


def _merge_kernel(x_ref, ya_ref, yb_ref, hf_ref, hb_ref, co_ref, mg_ref, mod_ref, g2_ref, gml_ref,
                  wb_ref, wout_ref, wrh_ref, wrl_ref, brt_ref, xo_ref, h2_ref, dw_ref):
    d = D_MODEL
    mod = mod_ref[...]
    hs = hf_ref[...].astype(F32) + hb_ref[...].astype(F32)
    parts = []
    for hd in range(ML_HEADS):
        v = hs[:, hd * ML_HEAD_DIM:(hd + 1) * ML_HEAD_DIM]
        parts.append(v * lax.rsqrt(jnp.mean(v * v, axis=-1, keepdims=True) + EPS))
    ym = jnp.concatenate(parts, axis=-1) * gml_ref[...] * _sigmoid(co_ref[...].astype(F32))
    merged = mg_ref[:, 0:d].astype(F32) * _dot_tn(ya_ref[...], wb_ref[0])
    merged = merged + mg_ref[:, d:2 * d].astype(F32) * _dot_tn(yb_ref[...], wb_ref[1])
    merged = merged + mg_ref[:, 2 * d:3 * d].astype(F32) * _dot(ym.astype(BF16), wb_ref[2])
    xn = x_ref[...] + mod[:, 2 * d:3 * d] * _dot(merged.astype(BF16), wout_ref[...])
    xo_ref[...] = xn
    h2 = _rms_mod(xn, g2_ref[...], mod[:, 3 * d:4 * d], mod[:, 4 * d:5 * d])
    h2_ref[...] = h2.astype(BF16)

    h_hi, h_lo = _split(h2)
    logits = _dot(h_hi, wrh_ref[...]) + _dot(h_hi, wrl_ref[...]) + _dot(h_lo, wrh_ref[...]) + brt_ref[...]
    lane = lax.broadcasted_iota(jnp.int32, logits.shape, 1).astype(F32)
    big = 1e9
    is_grp = (lane >= N_EXPERTS) & (lane < N_EXPERTS + N_GROUPS)
    gl = jnp.where(is_grp, logits, NEG_INF)
    gmax = jnp.max(gl, axis=-1, keepdims=True)
    gsel = jnp.min(jnp.where(gl == gmax, lane, big), axis=-1, keepdims=True) - N_EXPERTS
    p_grp = 1.0 / jnp.sum(jnp.exp(gl - gmax), axis=-1, keepdims=True)
    first = gsel * EXPERTS_PER_GROUP
    el = jnp.where((lane >= first) & (lane < first + EXPERTS_PER_GROUP), logits, NEG_INF)
    e1 = jnp.max(el, axis=-1, keepdims=True)
    i1 = jnp.min(jnp.where(el == e1, lane, big), axis=-1, keepdims=True)
    el2 = jnp.where(lane == i1, NEG_INF, el)
    e2 = jnp.max(el2, axis=-1, keepdims=True)
    i2 = jnp.min(jnp.where(el2 == e2, lane, big), axis=-1, keepdims=True)
    r = jnp.exp(e2 - e1)
    w1 = p_grp / (1.0 + r)
    dw_ref[...] = (jnp.where(lane == i1, w1, 0.0) + jnp.where(lane == i2, w1 * r, 0.0)
                   + jnp.where(lane == gsel + N_EXPERTS, 1.0, 0.0))


def _merge(xs, ya, yb, hf, hb, co, mg, mods, g2, gml, wb, wout, wrh, wrl, brt, nb, nt):
    t = xs.shape[0]
    d = D_MODEL
    row = lambda w: pl.BlockSpec((TILE, w), lambda i: (i, 0))
    full = lambda a: pl.BlockSpec(a.shape, lambda i: (0,) * a.ndim)
    return pl.pallas_call(
        _merge_kernel,
        grid=(t // TILE,),
        in_specs=[row(d), pl.BlockSpec((None, 512, TILE), lambda i: (i // nt, 0, i % nt)),
                  pl.BlockSpec((None, 512, TILE), lambda i: (i, 0, 0)),
                  row(512), row(512),
                  row(512), row(N_BRANCH * d),
                  pl.BlockSpec((None, 1, 6 * d), _tile_mod_index(nt, nb)),
                  full(g2), full(gml), full(wb), full(wout), full(wrh), full(wrl), full(brt)],
        out_specs=[row(d), row(d), row(LANES)],
        out_shape=[jax.ShapeDtypeStruct((t, d), F32), jax.ShapeDtypeStruct((t, d), BF16),
                   jax.ShapeDtypeStruct((t, LANES), F32)],
        compiler_params=_cparams(("arbitrary",), 48),
    )(xs, ya, yb, hf, hb, co, mg, mods, g2, gml, wb, wout, wrh, wrl, brt)


MOE_TILE = 1024
MOE_CHUNK = 128


def _moe_kernel(seg_ref, h_ref, dw_ref, wgu_ref, wd_ref, o_ref, hs_s, dws_s, acc_s, pt_s):
    i = pl.program_id(0)
    g = pl.program_id(1)
    tm = h_ref.shape[0]
    ng = N_GROUPS

    @pl.when(g == 0)
    def _():
        dw = dw_ref[...]
        lane = lax.broadcasted_iota(jnp.int32, dw.shape, 1)
        onehot = jnp.where((lane >= N_EXPERTS) & (lane < N_EXPERTS + ng), dw, 0.0)
        onehot_t = onehot.T
        row = lax.broadcasted_iota(jnp.int32, (tm, tm), 0)
        col = lax.broadcasted_iota(jnp.int32, (tm, tm), 1)
        rank_c = _dot(jnp.where(col < row, 1.0, 0.0).astype(BF16), onehot.astype(BF16))
        rank_r = _dot(onehot_t.astype(BF16), jnp.where(row < col, 1.0, 0.0).astype(BF16))
        sub = lax.broadcasted_iota(jnp.int32, onehot_t.shape, 0)
        lo_c = jnp.zeros(dw.shape, F32)
        lo_r = jnp.zeros(onehot_t.shape, F32)
        for k in range(ng):
            lo = seg_ref[(i * ng + k) * 2].astype(F32)
            lo_c = jnp.where(lane == N_EXPERTS + k, lo, lo_c)
            lo_r = jnp.where(sub == N_EXPERTS + k, lo, lo_r)
        pos_c = jnp.sum(onehot * (rank_c + lo_c), axis=1, keepdims=True)
        pos_r = jnp.sum(onehot_t * (rank_r + lo_r), axis=0, keepdims=True)
        perm = jnp.where(pos_r == row.astype(F32), 1.0, 0.0).astype(BF16)
        pt_s[...] = jnp.where(pos_c == col.astype(F32), 1.0, 0.0).astype(BF16)
        hs_s[...] = _dot(perm, h_ref[...]).astype(BF16)
        d1 = dw.astype(BF16)
        r1 = dw - d1.astype(F32)
        d2 = r1.astype(BF16)
        d3 = (r1 - d2.astype(F32)).astype(BF16)
        dws_s[...] = _dot(perm, d1) + _dot(perm, d2) + _dot(perm, d3)
        acc_s[...] = jnp.zeros_like(acc_s)

    lo = seg_ref[(i * ng + g) * 2]
    hi = seg_ref[(i * ng + g) * 2 + 1]
    c_lo = lo // MOE_CHUNK
    c_hi = jnp.where(hi > lo, (hi + MOE_CHUNK - 1) // MOE_CHUNK, c_lo)
    half = EXPERTS_PER_GROUP * D_EXPERT

    def chunk(c, carry):
        rows = pl.ds(pl.multiple_of(c * MOE_CHUNK, MOE_CHUNK), MOE_CHUNK)
        gu = _dot(hs_s[rows, :], wgu_ref[...])
        gate = gu[:, :half]
        act = gate * _sigmoid(gate) * gu[:, half:]
        dwc = dws_s[rows, :]
        lane = lax.broadcasted_iota(jnp.int32, dwc.shape, 1)
        scale = jnp.concatenate(
            [jnp.broadcast_to(jnp.sum(jnp.where(lane == g * EXPERTS_PER_GROUP + k, dwc, 0.0), axis=1, keepdims=True),
                              (MOE_CHUNK, D_EXPERT)) for k in range(EXPERTS_PER_GROUP)], axis=1)
        acc_s[rows, :] += _dot((act * scale).astype(BF16), wd_ref[...])
        return carry

    lax.fori_loop(c_lo, c_hi, chunk, 0)

    @pl.when(g == ng - 1)
    def _():
        o_ref[...] = _dot(pt_s[...], acc_s[...].astype(BF16)).astype(BF16)


def _moe(h2, dw, wgu, wd):
    t, d = h2.shape
    tm = MOE_TILE
    n_tiles = t // tm
    cnt = jnp.sum(dw[:, N_EXPERTS:N_EXPERTS + N_GROUPS].reshape(n_tiles, tm, N_GROUPS), axis=1).astype(jnp.int32)
    hi = jnp.cumsum(cnt, axis=1)
    seg = jnp.stack([hi - cnt, hi], axis=-1).reshape(-1)
    half = EXPERTS_PER_GROUP * D_EXPERT
    grid_spec = pltpu.PrefetchScalarGridSpec(
        num_scalar_prefetch=1,
        grid=(n_tiles, N_GROUPS),
        in_specs=[pl.BlockSpec((tm, d), lambda i, g, seg: (i, 0)),
                  pl.BlockSpec((tm, LANES), lambda i, g, seg: (i, 0)),
                  pl.BlockSpec((None, d, 2 * half), lambda i, g, seg: (g, 0, 0)),
                  pl.BlockSpec((None, half, d), lambda i, g, seg: (g, 0, 0))],
        out_specs=pl.BlockSpec((tm, d), lambda i, g, seg: (i, 0)),
        scratch_shapes=[pltpu.VMEM((tm, d), BF16), pltpu.VMEM((tm, LANES), F32),
                        pltpu.VMEM((tm, d), F32), pltpu.VMEM((tm, tm), BF16)],
    )
    return pl.pallas_call(
        _moe_kernel,
        grid_spec=grid_spec,
        out_shape=jax.ShapeDtypeStruct((t, d), BF16),
        compiler_params=_cparams(("arbitrary", "arbitrary"), 56),
    )(seg, h2, dw, wgu, wd)


def _group_weights(w_gate_up, w_down):
    d = w_gate_up.shape[1]
    wgu = w_gate_up.reshape(N_GROUPS, EXPERTS_PER_GROUP, d, 2, D_EXPERT).transpose(0, 2, 3, 1, 4)
    wgu = wgu.reshape(N_GROUPS, d, 2 * EXPERTS_PER_GROUP * D_EXPERT)
    wd = w_down.reshape(N_GROUPS, EXPERTS_PER_GROUP * D_EXPERT, d)
    return wgu.astype(BF16), wd.astype(BF16)


def _final_kernel(x_ref, y_ref, mod_ref, o_ref):
    d = D_MODEL
    o_ref[...] = x_ref[...] + mod_ref[:, 5 * d:6 * d] * y_ref[...].astype(F32)


def _final(xn, y, mods, nb, nt):
    d = D_MODEL
    blk = pl.BlockSpec((TILE, d), lambda b, j: (b * nt + 1 + j, 0))
    return pl.pallas_call(
        _final_kernel,
        grid=(nb, nt - 1),
        in_specs=[blk, blk, pl.BlockSpec((None, 1, 6 * d), lambda b, j: (b, 0, 0))],
        out_specs=pl.BlockSpec((None, TILE, d), lambda b, j: (b, j, 0)),
        out_shape=jax.ShapeDtypeStruct((nb, (nt - 1) * TILE, d), F32),
        compiler_params=_cparams(("arbitrary", "arbitrary"), 32),
    )(xn, y, mods)


def _rope_tables(n_tok, n_ctx):
    t = jnp.arange(n_tok)
    rowp = (t // GRID_W).astype(F32)
    colp = (t % GRID_W).astype(F32)
    n_freq = HEAD_DIM // 4
    inv = ROPE_THETA ** (-jnp.arange(n_freq, dtype=F32) / n_freq)
    ang = jnp.concatenate([rowp[:, None] * inv, colp[:, None] * inv], axis=-1)
    cos, sin = jnp.cos(ang), jnp.sin(ang)
    cos64 = jnp.concatenate([cos, cos], axis=-1)
    sin64 = jnp.concatenate([-sin, sin], axis=-1)
    pad = lambda a, v: jnp.concatenate([jnp.full((n_ctx, HEAD_DIM), v, F32), a], axis=0)
    return jnp.tile(pad(cos64, 1.0), (1, 4)), jnp.tile(pad(sin64, 0.0), (1, 4))


def kernel(x, c, ctx, c_ctx, w_mod, b_mod, g_norm, w_in, b_merge, g_qk, rpb, b_mlstm, g_ml, w_branch, w_out,
           w_group, b_group, w_router, b_router, w_gate_up, w_down):
    nb, n_tok, d = x.shape
    n_ctx = ctx.shape[1]
    depth = w_mod.shape[0]
    assert d == D_MODEL and n_ctx == TILE and n_tok % TILE == 0
    rows = n_tok // GRID_W
    nt = (n_ctx + n_tok) // TILE
    assert nt - 1 >= NA_WIN_TILES and rows >= NA_WIN_R
    t = nb * nt * TILE

    xs = jnp.concatenate([ctx, x], axis=1).reshape(t, d)
    n_mod = -(-(nb + 1) // 8) * 8
    cc = jnp.zeros((n_mod, d), F32).at[:nb].set(c).at[nb].set(c_ctx)
    mods_all = _modulation(cc, w_mod, b_mod)

    cos_t, sin_t = _rope_tables(n_tok, n_ctx)
    gmat = jnp.asarray(np.kron(np.eye(256 // HEAD_DIM), np.ones((HEAD_DIM, HEAD_DIM))), BF16)
    a_cols = np.concatenate([np.arange(h * HEAD_DIM, (h + 1) * HEAD_DIM) for h in _A_HEAD_ORDER])

    assert t % MOE_TILE == 0
    moe_y = mods_prev = None
    for l in range(depth):
        mods = mods_all[l].reshape(n_mod, 1, 6 * d)
        wl = w_in[l]
        w_in_p = jnp.concatenate(
            [wl[:, a_cols], wl[:, 512:4352], jnp.pad(wl[:, 4352:4368], ((0, 0), (0, GATE_W - 16))), wl[:, 4368:]],
            axis=1).astype(BF16)
        assert w_in_p.shape[1] == IN_WIDTH_P
        gqk = jnp.pad(jnp.tile(g_qk[l], (1, 256 // HEAD_DIM)), ((0, 4), (0, 0)))
        bcg = jnp.pad(b_mlstm[l].reshape(1, 16), ((0, 0), (0, GATE_W - 16)))
        bmg = b_merge[l].reshape(1, N_BRANCH * d)
        outs = _in_projection(xs, moe_y, mods_prev, mods, g_norm[l, 0:1], w_in_p, gmat, gqk, cos_t, sin_t,
                              bcg, bmg, nb, nt)
        if moe_y is not None:
            xs, *outs = outs
        qa, ka, vat, qb, kb, vbt, cq, ck, cv, co, cg, mg = outs

        ya = _attn_a(qa, ka, vat, nb, nt)
        yb = _attn_b(qb, kb, vbt, _na_table(rpb[l], rows), nb, nt)
        hf, hb = _mlstm(cq, ck, cv, cg, nb, nt)

        wb = jnp.stack([w_branch[l, 0][a_cols], w_branch[l, 1], w_branch[l, 2]]).astype(BF16)
        w_rt = jnp.concatenate([w_router[l], w_group[l], jnp.zeros((d, LANES - N_EXPERTS - N_GROUPS), F32)], axis=1)
        wrh = w_rt.astype(BF16)
        wrl = (w_rt - wrh.astype(F32)).astype(BF16)
        brt = jnp.concatenate([b_router[l], b_group[l], jnp.zeros((LANES - N_EXPERTS - N_GROUPS,), F32)])[None, :]
        xs, h2, dw = _merge(xs, ya, yb, hf, hb, co, mg, mods, g_norm[l, 1:2], g_ml[l][None, :], wb,
                            w_out[l].astype(BF16), wrh, wrl, brt, nb, nt)
        moe_y = _moe(h2, dw, *_group_weights(w_gate_up[l], w_down[l]))
        mods_prev = mods

    return _final(xs, moe_y, mods_prev, nb, nt)
```

```python
import functools

import numpy as np
import jax
import jax.numpy as jnp
from jax import lax
from jax.experimental import pallas as pl
from jax.experimental.pallas import tpu as pltpu

D_MODEL = 1024
GRID_W = 64
HEAD_DIM = 64
ROPE_THETA = 10000.0
A_HEADS = 8
A_KV_HEADS = 2
NA_HEADS = 8
NA_WIN_R = 8
NA_WIN_C = 16
ML_HEADS = 4
ML_HEAD_DIM = 128
N_BRANCH = 3
BRANCH_W = 512
N_GROUPS = 4
EXPERTS_PER_GROUP = 8
N_EXPERTS = N_GROUPS * EXPERTS_PER_GROUP
D_EXPERT = 256
EPS = 1e-6
NEG_INF = -1e30
LOG2E = 1.4426950408889634

TILE = 256
LANES = 128
NA_WIN_TILES = 3
GATE_W = 128
V7X_VMEM_BYTES = 64 * 1024 * 1024

F32 = jnp.float32
BF16 = jnp.bfloat16

_O_QA, _O_KA, _O_VA = 0, 512, 640
_O_QB, _O_KB, _O_VB = 768, 1280, 1792
_O_CQ, _O_CK, _O_CV, _O_CO = 2304, 2816, 3328, 3840
_O_CG = 4352
IN_WIDTH_P = _O_CG + GATE_W
_A_HEAD_ORDER = (0, 4, 1, 5, 2, 6, 3, 7)


def _cparams(sem, vmem_mb):
    return pltpu.CompilerParams(dimension_semantics=sem, vmem_limit_bytes=vmem_mb * 1024 * 1024)


def _dot(a, b):
    return jnp.dot(a, b, preferred_element_type=F32)


def _dot_nt(a, b):
    return lax.dot_general(a, b, (((1,), (1,)), ((), ())), preferred_element_type=F32)


def _dot_tn(a, b):
    return lax.dot_general(a, b, (((0,), (0,)), ((), ())), preferred_element_type=F32)


def _split(a):
    hi = a.astype(BF16)
    lo = (a - hi.astype(F32)).astype(BF16)
    return hi, lo


def _sigmoid(x):
    return 0.5 * jnp.tanh(0.5 * x) + 0.5


def _rms_mod(x, g, shift, scale):
    ms = jnp.mean(x * x, axis=-1, keepdims=True)
    y = x * lax.rsqrt(ms + EPS) * g
    return y * (1.0 + scale) + shift


def _mod_kernel(c_ref, w_ref, b_ref, o_ref):
    c = c_ref[...]
    a_hi, a_lo = _split(c * _sigmoid(c))
    w_hi, w_lo = _split(w_ref[...])
    o_ref[...] = _dot(a_hi, w_hi) + _dot(a_hi, w_lo) + _dot(a_lo, w_hi) + b_ref[...]


def _modulation(cc, w_mod, b_mod):
    depth, d, n = w_mod.shape
    r = cc.shape[0]
    bn = 1024
    return pl.pallas_call(
        _mod_kernel,
        grid=(depth, n // bn),
        in_specs=[
            pl.BlockSpec((r, d), lambda l, j: (0, 0)),
            pl.BlockSpec((None, d, bn), lambda l, j: (l, 0, j)),
            pl.BlockSpec((None, 1, bn), lambda l, j: (l, 0, j)),
        ],
        out_specs=pl.BlockSpec((None, r, bn), lambda l, j: (l, 0, j)),
        out_shape=jax.ShapeDtypeStruct((depth, r, n), F32),
        compiler_params=_cparams(("arbitrary", "arbitrary"), 32),
    )(cc, w_mod, b_mod.reshape(depth, 1, n))


def _group_norm64(acc, gmat, gain):
    hi, lo = _split(acc * acc)
    ss = _dot(hi, gmat) + _dot(lo, gmat)
    return acc * lax.rsqrt(ss * (1.0 / HEAD_DIM) + EPS) * gain


def _rope(x, cos, sin_signed):
    w = x.shape[1]
    lane = lax.broadcasted_iota(jnp.int32, x.shape, 1)
    nxt = pltpu.roll(x, w - HEAD_DIM // 2, 1)
    prv = pltpu.roll(x, HEAD_DIM // 2, 1)
    rot = jnp.where((lane % HEAD_DIM) < HEAD_DIM // 2, nxt, prv)
    return x * cos + rot * sin_signed


def _in_kernel(a_ref, b_ref, c_ref, mod_ref, g_ref, w_ref, gmat_ref, gqk_ref, cos_ref, sin_ref, bcg_ref,
               xo_ref, qa_ref, ka_ref, vat_ref, qb_ref, kb_ref, vbt_ref, cq_ref, ck_ref, cv_ref, co_ref, cg_ref,
               *, first, nt):
    d = D_MODEL
    if first:
        x = jnp.where(pl.program_id(0) % nt == 0, b_ref[...], a_ref[...])
    else:
        x = a_ref[...] + c_ref[:, 5 * d:6 * d] * b_ref[...].astype(F32)
    xo_ref[...] = x
    mod = mod_ref[...]
    h = _rms_mod(x, g_ref[...], mod[:, 0:d], mod[:, d:2 * d]).astype(BF16)

    def proj(a, width):
        return _dot(h, w_ref[:, a:a + width])

    gmat = gmat_ref[...]
    cos = cos_ref[...]
    sin = sin_ref[...]
    att_scale = HEAD_DIM ** -0.5
    for s in range(2):
        acc = _group_norm64(proj(_O_QA + 256 * s, 256), gmat, gqk_ref[0:1, :])
        qa_ref[:, 256 * s:256 * (s + 1)] = (_rope(acc, cos, sin) * (att_scale * LOG2E)).astype(BF16)
    acc = _group_norm64(proj(_O_KA, 128), gmat[:128, :128], gqk_ref[1:2, :128])
    ka_ref[...] = _rope(acc, cos[:, :128], sin[:, :128]).astype(BF16)
    vt = proj(_O_VA, 128).T
    ones = jnp.ones((HEAD_DIM, TILE), F32)
    vat_ref[...] = jnp.concatenate([vt[:HEAD_DIM], ones, vt[HEAD_DIM:], ones], axis=0).astype(BF16)
    for s in range(2):
        acc = _group_norm64(proj(_O_QB + 256 * s, 256), gmat, gqk_ref[2:3, :])
        qb_ref[:, 256 * s:256 * (s + 1)] = (acc * (att_scale * LOG2E)).astype(BF16)
        acc = _group_norm64(proj(_O_KB + 256 * s, 256), gmat, gqk_ref[3:4, :])
        kb_ref[:, 256 * s:256 * (s + 1)] = acc.astype(BF16)
    vbt = proj(_O_VB, 512).T
    ones = jnp.ones((NA_V_ROWS - HEAD_DIM, TILE), F32)
    vbt_ref[...] = jnp.concatenate(
        [piece for hd in range(NA_HEADS) for piece in (vbt[hd * HEAD_DIM:(hd + 1) * HEAD_DIM], ones)],
        axis=0).astype(BF16)
    cq_ref[...] = proj(_O_CQ, 512).astype(BF16)
    ck_ref[...] = (proj(_O_CK, 512) * (ML_HEAD_DIM ** -0.5)).astype(BF16)
    cv_ref[...] = proj(_O_CV, 512).astype(BF16)
    co_ref[...] = proj(_O_CO, 512).astype(BF16)
    cg_ref[...] = proj(_O_CG, GATE_W) + bcg_ref[...]


def _tile_mod_index(nt, nb):
    return lambda i: (jnp.where(i % nt == 0, nb, i // nt), 0, 0)


def _in_projection(stream, first, mods, g1, w_in_p, gmat, gqk, cos_t, sin_t, bcg, nb, nt):
    t = nb * nt * TILE
    d = D_MODEL
    row = lambda w: pl.BlockSpec((TILE, w), lambda i: (i, 0))
    full = lambda a: pl.BlockSpec(a.shape, lambda i: (0,) * a.ndim)
    mod_spec = pl.BlockSpec((None, 1, 6 * d), _tile_mod_index(nt, nb))
    if first:
        stream_specs = [pl.BlockSpec((None, TILE, d), lambda i: (i // nt, jnp.maximum(i % nt - 1, 0), 0)),
                        pl.BlockSpec((None, TILE, d), lambda i: (i // nt, 0, 0)), mod_spec]
    else:
        stream_specs = [row(d), row(d), mod_spec]
    act = jax.ShapeDtypeStruct((t, 512), BF16)
    vbt_rows = NA_HEADS * NA_V_ROWS
    out_shape = [jax.ShapeDtypeStruct((t, d), F32), act, jax.ShapeDtypeStruct((t, LANES), BF16),
                 jax.ShapeDtypeStruct((nb, 2 * LANES, nt * TILE), BF16),
                 act, act, jax.ShapeDtypeStruct((t // TILE, vbt_rows, TILE), BF16), act, act, act, act,
                 jax.ShapeDtypeStruct((t, GATE_W), F32)]
    out_specs = [row(d), row(512), row(LANES), pl.BlockSpec((None, 2 * LANES, TILE), lambda i: (i // nt, 0, i % nt)),
                 row(512), row(512), pl.BlockSpec((None, vbt_rows, TILE), lambda i: (i, 0, 0)),
                 row(512), row(512), row(512), row(512), row(GATE_W)]
    return pl.pallas_call(
        functools.partial(_in_kernel, first=first, nt=nt),
        grid=(t // TILE,),
        in_specs=stream_specs + [
            mod_spec,
            full(g1),
            pl.BlockSpec(w_in_p.shape, lambda i: (0, 0), pipeline_mode=pl.Buffered(1)),
            full(gmat),
            full(gqk),
            pl.BlockSpec((TILE, 256), lambda i: (i % nt, 0)),
            pl.BlockSpec((TILE, 256), lambda i: (i % nt, 0)),
            full(bcg),
        ],
        out_specs=out_specs,
        out_shape=out_shape,
        compiler_params=_cparams(("arbitrary",), 52),
    )(*stream, mods, g1, w_in_p, gmat, gqk, cos_t, sin_t, bcg)


def _attn_a_kernel(q_ref, k_ref, vt_ref, o_ref, s_scr):
    qt = pl.program_id(1)
    left = lax.broadcasted_iota(jnp.int32, (TILE, LANES), 1) < HEAD_DIM

    def query(hd):
        qblk = q_ref[:, (hd // 2) * LANES:(hd // 2 + 1) * LANES]
        zero = jnp.zeros_like(qblk)
        return jnp.where(left, qblk, zero) if hd % 2 == 0 else jnp.where(left, zero, qblk)

    def fold8(a):
        return functools.reduce(jnp.maximum, [a[8 * i:8 * (i + 1), :] for i in range(TILE // 8)])

    def run(n_chunks):
        rows = lambda c: slice(c * TILE, (c + 1) * TILE)

        def scores(hd, qh, c, m8):
            st = _dot_nt(k_ref[rows(c), :], qh)
            s_scr[hd % 2, rows(c), :] = st
            return jnp.maximum(m8, fold8(st))

        def values(hd, c, m, acc):
            p = jnp.exp2(s_scr[hd % 2, rows(c), :] - m).astype(BF16)
            kv = hd % 2
            return acc + _dot(vt_ref[kv * LANES:(kv + 1) * LANES, rows(c)], p)

        neg = jnp.full((8, TILE), NEG_INF, F32)
        qh = query(0)
        m8 = neg
        for c in range(n_chunks):
            m8 = scores(0, qh, c, m8)
        for hd in range(A_HEADS):
            m = jnp.max(m8, axis=0, keepdims=True)
            acc = jnp.zeros((LANES, TILE), F32)
            m8 = neg
            qh = query(hd + 1) if hd + 1 < A_HEADS else None
            for c in range(n_chunks):
                acc = values(hd, c, m, acc)
                if qh is not None:
                    m8 = scores(hd + 1, qh, c, m8)
            o_ref[hd * HEAD_DIM:(hd + 1) * HEAD_DIM, :] = (
                acc[0:HEAD_DIM, :] / acc[HEAD_DIM:HEAD_DIM + 1, :]).astype(BF16)

    @pl.when(qt == 0)
    def _():
        run(1)

    @pl.when(qt > 0)
    def _():
        run(k_ref.shape[0] // TILE)


def _attn_a(qa, ka, vat, nb, nt):
    s = nt * TILE
    w = A_HEADS * HEAD_DIM
    return pl.pallas_call(
        _attn_a_kernel,
        grid=(nb, nt),
        in_specs=[
            pl.BlockSpec((TILE, w), lambda b, i: (b * nt + i, 0)),
            pl.BlockSpec((None, s, LANES), lambda b, i: (b, 0, 0)),
            pl.BlockSpec((None, 2 * LANES, s), lambda b, i: (b, 0, 0)),
        ],
        out_specs=pl.BlockSpec((None, w, TILE), lambda b, i: (b, 0, i)),
        out_shape=jax.ShapeDtypeStruct((nb, w, s), BF16),
        scratch_shapes=[pltpu.VMEM((2, s, TILE), F32)],
        compiler_params=_cparams(("arbitrary",) * 2, 48),
    )(qa, ka.reshape(nb, s, LANES), vat)


NA_V_ROWS = HEAD_DIM + 16


def _attn_b_kernel(q_ref, k0_ref, k1_ref, k2_ref, kc_ref, v0_ref, v1_ref, v2_ref, vc_ref, tab_ref, o_ref, s_scr):
    left = lax.broadcasted_iota(jnp.int32, (TILE, LANES), 1) < HEAD_DIM
    k_refs = (k0_ref, k1_ref, k2_ref, kc_ref)
    v_refs = (v0_ref, v1_ref, v2_ref, vc_ref)
    n_blk = len(k_refs)

    def query(hd):
        qblk = q_ref[:, (hd // 2) * LANES:(hd // 2 + 1) * LANES]
        zero = jnp.zeros_like(qblk)
        return jnp.where(left, qblk, zero) if hd % 2 == 0 else jnp.where(left, zero, qblk)

    def fold8(a):
        return functools.reduce(jnp.maximum, [a[8 * i:8 * (i + 1), :] for i in range(TILE // 8)])

    def scores(hd, qh, j, m8):
        st = _dot_nt(k_refs[j][:, (hd // 2) * LANES:(hd // 2 + 1) * LANES], qh)
        if j < NA_WIN_TILES:
            st = st + tab_ref[hd, j * TILE:(j + 1) * TILE, :]
        s_scr[hd % 2, j] = st
        return jnp.maximum(m8, fold8(st))

    def values(hd, j, m, acc):
        p = jnp.exp2(s_scr[hd % 2, j] - m).astype(BF16)
        return acc + _dot(v_refs[j][hd * NA_V_ROWS:(hd + 1) * NA_V_ROWS, :], p)

    neg = jnp.full((8, TILE), NEG_INF, F32)
    qh = query(0)
    m8 = neg
    for j in range(n_blk):
        m8 = scores(0, qh, j, m8)
    for hd in range(NA_HEADS):
        m = jnp.max(m8, axis=0, keepdims=True)
        acc = jnp.zeros((NA_V_ROWS, TILE), F32)
        m8 = neg
        qh = query(hd + 1) if hd + 1 < NA_HEADS else None
        for j in range(n_blk):
            acc = values(hd, j, m, acc)
            if qh is not None:
                m8 = scores(hd + 1, qh, j, m8)
        o_ref[hd * HEAD_DIM:(hd + 1) * HEAD_DIM, :] = (
            acc[0:HEAD_DIM, :] / acc[HEAD_DIM:HEAD_DIM + 1, :]).astype(BF16)


def _attn_b(qb, kb, vbt, table, nb, nt):
    w = NA_HEADS * HEAD_DIM
    n_lat = nt - 1

    def win(j):
        return lambda i, b: b * nt + 1 + jnp.clip(i - 2, 0, n_lat - NA_WIN_TILES) + j

    ctx = lambda i, b: b * nt
    variant = lambda i, b: (jnp.where(i == 0, 3, jnp.where(i == 1, 0, jnp.where(i == n_lat, 2, 1))), 0, 0, 0)
    kblk = lambda f: pl.BlockSpec((TILE, w), lambda i, b: (f(i, b), 0))
    vblk = lambda f: pl.BlockSpec((None, NA_HEADS * NA_V_ROWS, TILE), lambda i, b: (f(i, b), 0, 0))
    return pl.pallas_call(
        _attn_b_kernel,
        grid=(nt, nb),
        in_specs=[kblk(lambda i, b: b * nt + i),
                  kblk(win(0)), kblk(win(1)), kblk(win(2)), kblk(ctx),
                  vblk(win(0)), vblk(win(1)), vblk(win(2)), vblk(ctx),
                  pl.BlockSpec((None, NA_HEADS, NA_WIN_TILES * TILE, TILE), variant)],
        out_specs=pl.BlockSpec((None, w, TILE), lambda i, b: (b * nt + i, 0, 0)),
        out_shape=jax.ShapeDtypeStruct((nb * nt, w, TILE), BF16),
        scratch_shapes=[pltpu.VMEM((2, NA_WIN_TILES + 1, TILE, TILE), F32)],
        compiler_params=_cparams(("arbitrary", "arbitrary"), 48),
    )(qb, kb, kb, kb, kb, vbt, vbt, vbt, vbt, table)


def _na_table(rpb, rows):
    rq = TILE // GRID_W
    n_lat = rows // rq
    wrows = NA_WIN_TILES * rq
    win_r = min(NA_WIN_R, rows)
    col = np.arange(GRID_W)
    col_start = np.clip(col - NA_WIN_C // 2, 0, GRID_W - NA_WIN_C)
    col_ok = (col[None, :] >= col_start[:, None]) & (col[None, :] < col_start[:, None] + NA_WIN_C)
    assert np.all(np.abs(col[None, :] - col[:, None])[col_ok] < NA_WIN_C)

    depth, nh, n_dr, n_dc = rpb.shape
    lpad = GRID_W - NA_WIN_C
    v = jnp.pad(rpb.astype(F32) * LOG2E, ((0, 0), (0, 0), (0, 0), (lpad, 2 * GRID_W - lpad - n_dc)))
    skew = jnp.tile(v, (1, 1, 1, GRID_W))[..., :GRID_W * (2 * GRID_W - 1)]
    toep = skew.reshape(depth, nh, n_dr, GRID_W, 2 * GRID_W - 1)[..., GRID_W - 1:]
    toep = jnp.pad(toep, ((0, 0), (0, 0), (wrows, wrows), (0, 0), (0, 0)))

    tabs = []
    for tq in (0, 1, n_lat - 1):
        r = tq * rq + np.arange(rq)
        kr = np.clip(tq - 1, 0, n_lat - NA_WIN_TILES) * rq + np.arange(wrows)
        rs = np.clip(r - win_r // 2, 0, rows - win_r)
        row_ok = (kr[None, :] >= rs[:, None]) & (kr[None, :] < rs[:, None] + win_r)
        dr0 = kr[0] - r + NA_WIN_R - 1
        assert np.all(dr0 + wrows >= 0) and np.all(dr0 <= n_dr)
        assert np.all((kr[None, :] - r[:, None] + NA_WIN_R - 1 >= 0)[row_ok])
        assert np.all((kr[None, :] - r[:, None] + NA_WIN_R - 1 < n_dr)[row_ok])
        ok = (row_ok[:, None, :, None] & col_ok[None, :, None, :]).reshape(TILE, wrows * GRID_W)
        blocks = jnp.stack([toep[:, :, a + wrows:a + 2 * wrows] for a in dr0], axis=2)
        bias = blocks.transpose(0, 1, 3, 5, 2, 4).reshape(depth, nh, wrows * GRID_W, TILE)
        tabs.append(jnp.where(ok.T[None, None], bias, NEG_INF))
    tabs.append(jnp.full_like(tabs[0], NEG_INF))
    return jnp.stack(tabs, axis=1)


def _log_sigmoid(x):
    return jnp.minimum(x, 0.0) - jnp.log(1.0 + jnp.exp(-jnp.abs(x)))


def _mlstm_kernel(qf_ref, kf_ref, vf_ref, gf_ref, qb_ref, kb_ref, vb_ref, gb_ref, of_ref, ob_ref, c_s, n_s, m_s):
    step = pl.program_id(1)

    @pl.when(step == 0)
    def _():
        c_s[...] = jnp.zeros_like(c_s)
        n_s[...] = jnp.zeros_like(n_s)
        m_s[...] = jnp.zeros_like(m_s)

    n = TILE
    row = lax.broadcasted_iota(jnp.int32, (n, n), 0)
    col = lax.broadcasted_iota(jnp.int32, (n, n), 1)
    lower, upper = row >= col, row <= col
    dirs = ((qf_ref, kf_ref, vf_ref, gf_ref, of_ref, lower, upper), (qb_ref, kb_ref, vb_ref, gb_ref, ob_ref, upper, lower))
    for direction, (q_ref, k_ref, v_ref, g_ref, o_ref, seen, seen_t) in enumerate(dirs):
        seen_bf = jnp.where(seen, 1.0, 0.0).astype(BF16)
        seen_t_bf = jnp.where(seen_t, 1.0, 0.0).astype(BF16)
        gates = g_ref[...]
        logf = _log_sigmoid(gates)
        gates_t = gates.T
        lf_hi, lf_lo = _split(logf)
        cum_c = _dot(seen_bf, lf_hi) + _dot(seen_bf, lf_lo)
        lft_hi, lft_lo = _split(logf.T)
        cum_r = _dot(lft_hi, seen_t_bf) + _dot(lft_lo, seen_t_bf)
        tot = jnp.sum(logf, axis=0, keepdims=True)

        for hd in range(ML_HEADS):
            ci = 8 * direction + hd
            cf = ci + 4
            st = direction * ML_HEADS + hd
            ig_c = gates[:, ci:ci + 1]
            b_c = cum_c[:, cf:cf + 1]
            ig_r = gates_t[ci:ci + 1, :]
            b_r = cum_r[cf:cf + 1, :]
            b_end = tot[:, cf:cf + 1]
            m_prev = m_s[st, 0:1, 0:1]
            n_prev = n_s[st, 0:1, :]
            c_prev = c_s[st]
            sl = slice(hd * ML_HEAD_DIM, (hd + 1) * ML_HEAD_DIM)
            q = q_ref[:, sl]
            k = k_ref[:, sl]
            v = v_ref[:, sl]

            a = b_c + m_prev
            dmat = jnp.where(seen, b_c - b_r + ig_r, NEG_INF)
            m_t = jnp.maximum(a, jnp.max(dmat, axis=-1, keepdims=True))
            w_inter = jnp.exp(a - m_t)
            smat = _dot_nt(q, k) * jnp.exp(dmat - m_t)
            num = w_inter * _dot(q, c_prev.astype(BF16)) + _dot(smat.astype(BF16), v)
            qn = jnp.sum(q.astype(F32) * n_prev, axis=-1, keepdims=True)
            den = w_inter * qn + jnp.sum(smat, axis=-1, keepdims=True)
            o_ref[:, sl] = (num / jnp.maximum(jnp.abs(den), jnp.exp(-m_t))).astype(BF16)

            log_w = b_end - b_c + ig_c
            m_new = jnp.maximum(b_end + m_prev, jnp.max(log_w, axis=0, keepdims=True))
            g_inter = jnp.exp(b_end + m_prev - m_new)
            kg = k.astype(F32) * jnp.exp(log_w - m_new)
            c_s[st] = g_inter * c_prev + _dot_tn(kg.astype(BF16), v)
            n_s[st] = jnp.broadcast_to(g_inter * n_prev + jnp.sum(kg, axis=0, keepdims=True), n_s.shape[1:])
            m_s[st] = jnp.broadcast_to(m_new, m_s.shape[1:])


def _mlstm(cq, ck, cv, cg, nb, nt):
    t = cq.shape[0]
    w = ML_HEADS * ML_HEAD_DIM
    fwd = lambda b, s: (b * nt + s, 0)
    bwd = lambda b, s: (b * nt + jnp.where(s == 0, 0, nt - s), 0)
    blk = lambda width, f: pl.BlockSpec((TILE, width), f)
    n_state = 2 * ML_HEADS
    return pl.pallas_call(
        _mlstm_kernel,
        grid=(nb, nt),
        in_specs=[blk(w, fwd), blk(w, fwd), blk(w, fwd), blk(GATE_W, fwd),
                  blk(w, bwd), blk(w, bwd), blk(w, bwd), blk(GATE_W, bwd)],
        out_specs=[blk(w, fwd), blk(w, bwd)],
        out_shape=[jax.ShapeDtypeStruct((t, w), BF16)] * 2,
        scratch_shapes=[pltpu.VMEM((n_state, ML_HEAD_DIM, ML_HEAD_DIM), F32),
                        pltpu.VMEM((n_state, 8, ML_HEAD_DIM), F32),
                        pltpu.VMEM((n_state, 8, LANES), F32)],
        compiler_params=_cparams(("arbitrary",) * 2, 32),
    )(cq, ck, cv, cg, cq, ck, cv, cg)


def _merge_kernel(x_ref, ya_ref, yb_ref, hf_ref, hb_ref, co_ref, mod_ref, g1_ref, g2_ref, gml_ref,
                  wmg_ref, bmg_ref, wb_ref, wout_ref, wrh_ref, wrl_ref, brt_ref, xo_ref, h2_ref, dw_ref):
    d = D_MODEL
    mod = mod_ref[...]
    x = x_ref[...]
    h1 = _rms_mod(x, g1_ref[...], mod[:, 0:d], mod[:, d:2 * d]).astype(BF16)

    def gate(br):
        return _sigmoid(_dot(h1, wmg_ref[:, br * d:(br + 1) * d]) + bmg_ref[:, br * d:(br + 1) * d])

    hs = hf_ref[...].astype(F32) + hb_ref[...].astype(F32)
    parts = []
    for hd in range(ML_HEADS):
        v = hs[:, hd * ML_HEAD_DIM:(hd + 1) * ML_HEAD_DIM]
        parts.append(v * lax.rsqrt(jnp.mean(v * v, axis=-1, keepdims=True) + EPS))
    ym = jnp.concatenate(parts, axis=-1) * gml_ref[...] * _sigmoid(co_ref[...].astype(F32))
    merged = gate(0) * _dot_tn(ya_ref[...], wb_ref[0])
    merged = merged + gate(1) * _dot_tn(yb_ref[...], wb_ref[1])
    merged = merged + gate(2) * _dot(ym.astype(BF16), wb_ref[2])
    xn = x + mod[:, 2 * d:3 * d] * _dot(merged.astype(BF16), wout_ref[...])
    xo_ref[...] = xn
    h2 = _rms_mod(xn, g2_ref[...], mod[:, 3 * d:4 * d], mod[:, 4 * d:5 * d])
    h2_ref[...] = h2.astype(BF16)

    h_hi, h_lo = _split(h2)
    logits = _dot(h_hi, wrh_ref[...]) + _dot(h_hi, wrl_ref[...]) + _dot(h_lo, wrh_ref[...]) + brt_ref[...]
    lane = lax.broadcasted_iota(jnp.int32, logits.shape, 1).astype(F32)
    big = 1e9
    is_grp = (lane >= N_EXPERTS) & (lane < N_EXPERTS + N_GROUPS)
    gl = jnp.where(is_grp, logits, NEG_INF)
    gmax = jnp.max(gl, axis=-1, keepdims=True)
    gsel = jnp.min(jnp.where(gl == gmax, lane, big), axis=-1, keepdims=True) - N_EXPERTS
    p_grp = 1.0 / jnp.sum(jnp.exp(gl - gmax), axis=-1, keepdims=True)
    first = gsel * EXPERTS_PER_GROUP
    el = jnp.where((lane >= first) & (lane < first + EXPERTS_PER_GROUP), logits, NEG_INF)
    e1 = jnp.max(el, axis=-1, keepdims=True)
    i1 = jnp.min(jnp.where(el == e1, lane, big), axis=-1, keepdims=True)
    el2 = jnp.where(lane == i1, NEG_INF, el)
    e2 = jnp.max(el2, axis=-1, keepdims=True)
    i2 = jnp.min(jnp.where(el2 == e2, lane, big), axis=-1, keepdims=True)
    r = jnp.exp(e2 - e1)
    w1 = p_grp / (1.0 + r)
    dw_ref[...] = (jnp.where(lane == i1, w1, 0.0) + jnp.where(lane == i2, w1 * r, 0.0)
                   + jnp.where(lane == gsel + N_EXPERTS, 1.0, 0.0))


def _merge(xs, ya, yb, hf, hb, co, mods, g1, g2, gml, wmg, bmg, wb, wout, wrh, wrl, brt, nb, nt):
    t = xs.shape[0]
    d = D_MODEL
    row = lambda w: pl.BlockSpec((TILE, w), lambda i: (i, 0))
    full = lambda a: pl.BlockSpec(a.shape, lambda i: (0,) * a.ndim)
    return pl.pallas_call(
        _merge_kernel,
        grid=(t // TILE,),
        in_specs=[row(d), pl.BlockSpec((None, 512, TILE), lambda i: (i // nt, 0, i % nt)),
                  pl.BlockSpec((None, 512, TILE), lambda i: (i, 0, 0)),
                  row(512), row(512), row(512),
                  pl.BlockSpec((None, 1, 6 * d), _tile_mod_index(nt, nb)),
                  full(g1), full(g2), full(gml), full(wmg), full(bmg), full(wb), full(wout),
                  full(wrh), full(wrl), full(brt)],
        out_specs=[row(d), row(d), row(LANES)],
        out_shape=[jax.ShapeDtypeStruct((t, d), F32), jax.ShapeDtypeStruct((t, d), BF16),
                   jax.ShapeDtypeStruct((t, LANES), F32)],
        compiler_params=_cparams(("arbitrary",), 48),
    )(xs, ya, yb, hf, hb, co, mods, g1, g2, gml, wmg, bmg, wb, wout, wrh, wrl, brt)


MOE_TILE = 1024
MOE_CHUNK = 128


def _moe_kernel(seg_ref, h_ref, dw_ref, wgu_ref, wd_ref, o_ref, hs_s, dws_s, acc_s, pt_s):
    i = pl.program_id(0)
    g = pl.program_id(1)
    tm = h_ref.shape[0]
    ng = N_GROUPS

    @pl.when(g == 0)
    def _():
        dw = dw_ref[...]
        lane = lax.broadcasted_iota(jnp.int32, dw.shape, 1)
        onehot = jnp.where((lane >= N_EXPERTS) & (lane < N_EXPERTS + ng), dw, 0.0)
        onehot_t = onehot.T
        row = lax.broadcasted_iota(jnp.int32, (tm, tm), 0)
        col = lax.broadcasted_iota(jnp.int32, (tm, tm), 1)
        rank_c = _dot(jnp.where(col < row, 1.0, 0.0).astype(BF16), onehot.astype(BF16))
        rank_r = _dot(onehot_t.astype(BF16), jnp.where(row < col, 1.0, 0.0).astype(BF16))
        sub = lax.broadcasted_iota(jnp.int32, onehot_t.shape, 0)
        lo_c = jnp.zeros(dw.shape, F32)
        lo_r = jnp.zeros(onehot_t.shape, F32)
        for k in range(ng):
            lo = seg_ref[(i * ng + k) * 2].astype(F32)
            lo_c = jnp.where(lane == N_EXPERTS + k, lo, lo_c)
            lo_r = jnp.where(sub == N_EXPERTS + k, lo, lo_r)
        pos_c = jnp.sum(onehot * (rank_c + lo_c), axis=1, keepdims=True)
        pos_r = jnp.sum(onehot_t * (rank_r + lo_r), axis=0, keepdims=True)
        perm = jnp.where(pos_r == row.astype(F32), 1.0, 0.0).astype(BF16)
        pt_s[...] = jnp.where(pos_c == col.astype(F32), 1.0, 0.0).astype(BF16)
        hs_s[...] = _dot(perm, h_ref[...]).astype(BF16)
        d1 = dw.astype(BF16)
        r1 = dw - d1.astype(F32)
        d2 = r1.astype(BF16)
        d3 = (r1 - d2.astype(F32)).astype(BF16)
        dws_s[...] = _dot(perm, d1) + _dot(perm, d2) + _dot(perm, d3)
        acc_s[...] = jnp.zeros_like(acc_s)

    lo = seg_ref[(i * ng + g) * 2]
    hi = seg_ref[(i * ng + g) * 2 + 1]
    c_lo = lo // MOE_CHUNK
    c_hi = jnp.where(hi > lo, (hi + MOE_CHUNK - 1) // MOE_CHUNK, c_lo)
    def chunk(c, carry):
        rows = pl.ds(pl.multiple_of(c * MOE_CHUNK, MOE_CHUNK), MOE_CHUNK)
        hc = hs_s[rows, :]
        dwc = dws_s[rows, :]
        lane = lax.broadcasted_iota(jnp.int32, dwc.shape, 1)
        acts = []
        for k in range(EXPERTS_PER_GROUP):
            gu = _dot(hc, wgu_ref[k])
            gate = gu[:, :D_EXPERT]
            wk = jnp.sum(jnp.where(lane == g * EXPERTS_PER_GROUP + k, dwc, 0.0), axis=1, keepdims=True)
            acts.append((gate * _sigmoid(gate) * gu[:, D_EXPERT:] * wk).astype(BF16))
        wd = wd_ref[...].reshape(EXPERTS_PER_GROUP * D_EXPERT, wd_ref.shape[2])
        acc_s[rows, :] += _dot(jnp.concatenate(acts, axis=1), wd)
        return carry

    lax.fori_loop(c_lo, c_hi, chunk, 0)

    @pl.when(g == ng - 1)
    def _():
        o_ref[...] = _dot(pt_s[...], acc_s[...].astype(BF16)).astype(BF16)


def _moe(h2, dw, wgu, wd):
    t, d = h2.shape
    tm = MOE_TILE
    n_tiles = t // tm
    cnt = jnp.sum(dw[:, N_EXPERTS:N_EXPERTS + N_GROUPS].reshape(n_tiles, tm, N_GROUPS), axis=1).astype(jnp.int32)
    hi = jnp.cumsum(cnt, axis=1)
    seg = jnp.stack([hi - cnt, hi], axis=-1).reshape(-1)
    grid_spec = pltpu.PrefetchScalarGridSpec(
        num_scalar_prefetch=1,
        grid=(n_tiles, N_GROUPS),
        in_specs=[pl.BlockSpec((tm, d), lambda i, g, seg: (i, 0)),
                  pl.BlockSpec((tm, LANES), lambda i, g, seg: (i, 0)),
                  pl.BlockSpec((EXPERTS_PER_GROUP, d, 2 * D_EXPERT), lambda i, g, seg: (g, 0, 0)),
                  pl.BlockSpec((EXPERTS_PER_GROUP, D_EXPERT, d), lambda i, g, seg: (g, 0, 0))],
        out_specs=pl.BlockSpec((tm, d), lambda i, g, seg: (i, 0)),
        scratch_shapes=[pltpu.VMEM((tm, d), BF16), pltpu.VMEM((tm, LANES), F32),
                        pltpu.VMEM((tm, d), F32), pltpu.VMEM((tm, tm), BF16)],
    )
    return pl.pallas_call(
        _moe_kernel,
        grid_spec=grid_spec,
        out_shape=jax.ShapeDtypeStruct((t, d), BF16),
        compiler_params=_cparams(("arbitrary", "arbitrary"), 56),
    )(seg, h2, dw, wgu, wd)


def _final_kernel(x_ref, y_ref, mod_ref, o_ref):
    d = D_MODEL
    o_ref[...] = x_ref[...] + mod_ref[:, 5 * d:6 * d] * y_ref[...].astype(F32)


def _final(xn, y, mods, nb, nt):
    d = D_MODEL
    blk = pl.BlockSpec((TILE, d), lambda b, j: (b * nt + 1 + j, 0))
    return pl.pallas_call(
        _final_kernel,
        grid=(nb, nt - 1),
        in_specs=[blk, blk, pl.BlockSpec((None, 1, 6 * d), lambda b, j: (b, 0, 0))],
        out_specs=pl.BlockSpec((None, TILE, d), lambda b, j: (b, j, 0)),
        out_shape=jax.ShapeDtypeStruct((nb, (nt - 1) * TILE, d), F32),
        compiler_params=_cparams(("arbitrary", "arbitrary"), 32),
    )(xn, y, mods)


def _rope_tables(n_tok, n_ctx):
    t = jnp.arange(n_tok)
    rowp = (t // GRID_W).astype(F32)
    colp = (t % GRID_W).astype(F32)
    n_freq = HEAD_DIM // 4
    inv = ROPE_THETA ** (-jnp.arange(n_freq, dtype=F32) / n_freq)
    ang = jnp.concatenate([rowp[:, None] * inv, colp[:, None] * inv], axis=-1)
    cos, sin = jnp.cos(ang), jnp.sin(ang)
    cos64 = jnp.concatenate([cos, cos], axis=-1)
    sin64 = jnp.concatenate([-sin, sin], axis=-1)
    pad = lambda a, v: jnp.concatenate([jnp.full((n_ctx, HEAD_DIM), v, F32), a], axis=0)
    return jnp.tile(pad(cos64, 1.0), (1, 4)), jnp.tile(pad(sin64, 0.0), (1, 4))


def kernel(x, c, ctx, c_ctx, w_mod, b_mod, g_norm, w_in, b_merge, g_qk, rpb, b_mlstm, g_ml, w_branch, w_out,
           w_group, b_group, w_router, b_router, w_gate_up, w_down):
    nb, n_tok, d = x.shape
    n_ctx = ctx.shape[1]
    depth = w_mod.shape[0]
    assert d == D_MODEL and n_ctx == TILE and n_tok % TILE == 0
    rows = n_tok // GRID_W
    nt = (n_ctx + n_tok) // TILE
    assert nt - 1 >= NA_WIN_TILES and rows >= NA_WIN_R
    t = nb * nt * TILE

    n_mod = -(-(nb + 1) // 8) * 8
    cc = jnp.zeros((n_mod, d), F32).at[:nb].set(c).at[nb].set(c_ctx)
    mods_all = _modulation(cc, w_mod, b_mod)

    cos_t, sin_t = _rope_tables(n_tok, n_ctx)
    gmat = jnp.asarray(np.kron(np.eye(256 // HEAD_DIM), np.ones((HEAD_DIM, HEAD_DIM))), BF16)
    a_cols = np.concatenate([np.arange(h * HEAD_DIM, (h + 1) * HEAD_DIM) for h in _A_HEAD_ORDER])
    na_tables = _na_table(rpb, rows)

    assert t % MOE_TILE == 0
    stream = None
    for l in range(depth):
        mods = mods_all[l].reshape(n_mod, 1, 6 * d)
        wl = w_in[l]
        w_in_p = jnp.concatenate(
            [wl[:, a_cols], wl[:, 512:4352], jnp.pad(wl[:, 4352:4368], ((0, 0), (0, GATE_W - 16)))],
            axis=1).astype(BF16)
        assert w_in_p.shape[1] == IN_WIDTH_P
        wmg = wl[:, 4368:].astype(BF16)
        gqk = jnp.pad(jnp.tile(g_qk[l], (1, 256 // HEAD_DIM)), ((0, 4), (0, 0)))
        bcg = jnp.pad(b_mlstm[l].reshape(1, 16), ((0, 0), (0, GATE_W - 16)))
        bmg = b_merge[l].reshape(1, N_BRANCH * d)
        first = stream is None
        xs, qa, ka, vat, qb, kb, vbt, cq, ck, cv, co, cg = _in_projection(
            (x, ctx, mods) if first else stream, first, mods, g_norm[l, 0:1], w_in_p, gmat, gqk, cos_t, sin_t,
            bcg, nb, nt)

        ya = _attn_a(qa, ka, vat, nb, nt)
        yb = _attn_b(qb, kb, vbt, na_tables[l], nb, nt)
        hf, hb = _mlstm(cq, ck, cv, cg, nb, nt)

        wb = jnp.stack([w_branch[l, 0][a_cols], w_branch[l, 1], w_branch[l, 2]]).astype(BF16)
        w_rt = jnp.concatenate([w_router[l], w_group[l], jnp.zeros((d, LANES - N_EXPERTS - N_GROUPS), F32)], axis=1)
        wrh = w_rt.astype(BF16)
        wrl = (w_rt - wrh.astype(F32)).astype(BF16)
        brt = jnp.concatenate([b_router[l], b_group[l], jnp.zeros((LANES - N_EXPERTS - N_GROUPS,), F32)])[None, :]
        xs, h2, dw = _merge(xs, ya, yb, hf, hb, co, mods, g_norm[l, 0:1], g_norm[l, 1:2], g_ml[l][None, :],
                            wmg, bmg, wb, w_out[l].astype(BF16), wrh, wrl, brt, nb, nt)
        stream = (xs, _moe(h2, dw, w_gate_up[l].astype(BF16), w_down[l].astype(BF16)), mods)

    return _final(*stream, nb, nt)
```

```python
import functools

import numpy as np
import jax
import jax.numpy as jnp
from jax import lax
from jax.experimental import pallas as pl
from jax.experimental.pallas import tpu as pltpu

D_MODEL = 1024
GRID_W = 64
HEAD_DIM = 64
ROPE_THETA = 10000.0
A_HEADS = 8
A_KV_HEADS = 2
NA_HEADS = 8
NA_WIN_R = 8
NA_WIN_C = 16
ML_HEADS = 4
ML_HEAD_DIM = 128
N_BRANCH = 3
BRANCH_W = 512
N_GROUPS = 4
EXPERTS_PER_GROUP = 8
N_EXPERTS = N_GROUPS * EXPERTS_PER_GROUP
D_EXPERT = 256
EPS = 1e-6
NEG_INF = -1e30
LOG2E = 1.4426950408889634

TILE = 256
LANES = 128
NA_WIN_TILES = 3
GATE_W = 128
V7X_VMEM_BYTES = 64 * 1024 * 1024

F32 = jnp.float32
BF16 = jnp.bfloat16

_O_QA, _O_KA, _O_VA = 0, 512, 640
_O_QB, _O_KB, _O_VB = 768, 1280, 1792
_O_CQ, _O_CK, _O_CV, _O_CO = 2304, 2816, 3328, 3840
_O_CG = 4352
IN_WIDTH_P = _O_CG + GATE_W
_A_HEAD_ORDER = (0, 4, 1, 5, 2, 6, 3, 7)


def _cparams(sem, vmem_mb):
    return pltpu.CompilerParams(dimension_semantics=sem, vmem_limit_bytes=vmem_mb * 1024 * 1024)


def _dot(a, b):
    return jnp.dot(a, b, preferred_element_type=F32)


def _dot_nt(a, b):
    return lax.dot_general(a, b, (((1,), (1,)), ((), ())), preferred_element_type=F32)


def _dot_tn(a, b):
    return lax.dot_general(a, b, (((0,), (0,)), ((), ())), preferred_element_type=F32)


def _split(a):
    hi = a.astype(BF16)
    lo = (a - hi.astype(F32)).astype(BF16)
    return hi, lo


def _sigmoid(x):
    return 0.5 * jnp.tanh(0.5 * x) + 0.5


def _rms_mod(x, g, shift, scale):
    ms = jnp.mean(x * x, axis=-1, keepdims=True)
    y = x * lax.rsqrt(ms + EPS) * g
    return y * (1.0 + scale) + shift


def _mod_kernel(c_ref, w_ref, b_ref, o_ref):
    c = c_ref[...]
    a_hi, a_lo = _split(c * _sigmoid(c))
    w_hi, w_lo = _split(w_ref[...])
    o_ref[...] = _dot(a_hi, w_hi) + _dot(a_hi, w_lo) + _dot(a_lo, w_hi) + b_ref[...]


def _modulation(cc, w_mod, b_mod):
    depth, d, n = w_mod.shape
    r = cc.shape[0]
    bn = 1024
    return pl.pallas_call(
        _mod_kernel,
        grid=(depth, n // bn),
        in_specs=[
            pl.BlockSpec((r, d), lambda l, j: (0, 0)),
            pl.BlockSpec((None, d, bn), lambda l, j: (l, 0, j)),
            pl.BlockSpec((None, 1, bn), lambda l, j: (l, 0, j)),
        ],
        out_specs=pl.BlockSpec((None, r, bn), lambda l, j: (l, 0, j)),
        out_shape=jax.ShapeDtypeStruct((depth, r, n), F32),
        compiler_params=_cparams(("arbitrary", "arbitrary"), 32),
    )(cc, w_mod, b_mod.reshape(depth, 1, n))


def _group_norm64(acc, gmat, gain):
    ss = _dot((acc * acc).astype(BF16), gmat)
    return acc * lax.rsqrt(ss * (1.0 / HEAD_DIM) + EPS) * gain


def _rope(x, cos, sin_signed):
    w = x.shape[1]
    lane = lax.broadcasted_iota(jnp.int32, x.shape, 1)
    nxt = pltpu.roll(x, w - HEAD_DIM // 2, 1)
    prv = pltpu.roll(x, HEAD_DIM // 2, 1)
    rot = jnp.where((lane % HEAD_DIM) < HEAD_DIM // 2, nxt, prv)
    return x * cos + rot * sin_signed


def _in_kernel(a_ref, b_ref, c_ref, mod_ref, g_ref, w_ref, gmat_ref, gqk_ref, cos_ref, sin_ref, bcg_ref,
               xo_ref, qa_ref, ka_ref, vat_ref, qb_ref, kb_ref, vbt_ref, cq_ref, ck_ref, cv_ref, co_ref, cg_ref,
               *, first, nt):
    d = D_MODEL
    if first:
        x = jnp.where(pl.program_id(0) % nt == 0, b_ref[...], a_ref[...])
    else:
        x = a_ref[...] + c_ref[:, 5 * d:6 * d] * b_ref[...].astype(F32)
    xo_ref[...] = x
    mod = mod_ref[...]
    h = _rms_mod(x, g_ref[...], mod[:, 0:d], mod[:, d:2 * d]).astype(BF16)

    def proj(a, width):
        return _dot(h, w_ref[:, a:a + width])

    gmat = gmat_ref[...]
    cos = cos_ref[...]
    sin = sin_ref[...]
    att_scale = HEAD_DIM ** -0.5
    pa = proj(_O_QA, _O_QB - _O_QA)
    for s in range(2):
        acc = _group_norm64(pa[:, 256 * s:256 * (s + 1)], gmat, gqk_ref[0:1, :])
        qa_ref[:, 256 * s:256 * (s + 1)] = (_rope(acc, cos, sin) * (att_scale * LOG2E)).astype(BF16)
    acc = _group_norm64(pa[:, _O_KA:_O_VA], gmat[:128, :128], gqk_ref[1:2, :128])
    ka_ref[...] = _rope(acc, cos[:, :128], sin[:, :128]).astype(BF16)
    vt = pa[:, _O_VA:_O_QB].T
    ones = jnp.ones((HEAD_DIM, TILE), F32)
    vat_ref[...] = jnp.concatenate([vt[:HEAD_DIM], ones, vt[HEAD_DIM:], ones], axis=0).astype(BF16)
    pb = proj(_O_QB, _O_VB - _O_QB)
    for s in range(2):
        acc = _group_norm64(pb[:, 256 * s:256 * (s + 1)], gmat, gqk_ref[2:3, :])
        qb_ref[:, 256 * s:256 * (s + 1)] = (acc * (att_scale * LOG2E)).astype(BF16)
        acc = _group_norm64(pb[:, 512 + 256 * s:512 + 256 * (s + 1)], gmat, gqk_ref[3:4, :])
        kb_ref[:, 256 * s:256 * (s + 1)] = acc.astype(BF16)
    vbt = proj(_O_VB, 512).T
    ones = jnp.ones((NA_V_ROWS - HEAD_DIM, TILE), F32)
    vbt_ref[...] = jnp.concatenate(
        [piece for hd in range(NA_HEADS) for piece in (vbt[hd * HEAD_DIM:(hd + 1) * HEAD_DIM], ones)],
        axis=0).astype(BF16)
    pc = proj(_O_CQ, IN_WIDTH_P - _O_CQ)
    cq_ref[...] = pc[:, 0:512].astype(BF16)
    ck_ref[...] = (pc[:, 512:1024] * (ML_HEAD_DIM ** -0.5)).astype(BF16)
    cv_ref[...] = pc[:, 1024:1536].astype(BF16)
    co_ref[...] = pc[:, 1536:2048].astype(BF16)
    cg_ref[...] = pc[:, 2048:2048 + GATE_W] + bcg_ref[...]


def _tile_mod_index(nt, nb):
    return lambda i: (jnp.where(i % nt == 0, nb, i // nt), 0, 0)


def _in_projection(stream, first, mods, g1, w_in_p, gmat, gqk, cos_t, sin_t, bcg, nb, nt):
    t = nb * nt * TILE
    d = D_MODEL
    row = lambda w: pl.BlockSpec((TILE, w), lambda i: (i, 0))
    full = lambda a: pl.BlockSpec(a.shape, lambda i: (0,) * a.ndim)
    mod_spec = pl.BlockSpec((None, 1, 6 * d), _tile_mod_index(nt, nb))
    if first:
        stream_specs = [pl.BlockSpec((None, TILE, d), lambda i: (i // nt, jnp.maximum(i % nt - 1, 0), 0)),
                        pl.BlockSpec((None, TILE, d), lambda i: (i // nt, 0, 0)), mod_spec]
    else:
        stream_specs = [row(d), row(d), mod_spec]
    act = jax.ShapeDtypeStruct((t, 512), BF16)
    vbt_rows = NA_HEADS * NA_V_ROWS
    out_shape = [jax.ShapeDtypeStruct((t, d), F32), act, jax.ShapeDtypeStruct((t, LANES), BF16),
                 jax.ShapeDtypeStruct((nb, 2 * LANES, nt * TILE), BF16),
                 act, act, jax.ShapeDtypeStruct((t // TILE, vbt_rows, TILE), BF16), act, act, act, act,
                 jax.ShapeDtypeStruct((t, GATE_W), F32)]
    out_specs = [row(d), row(512), row(LANES), pl.BlockSpec((None, 2 * LANES, TILE), lambda i: (i // nt, 0, i % nt)),
                 row(512), row(512), pl.BlockSpec((None, vbt_rows, TILE), lambda i: (i, 0, 0)),
                 row(512), row(512), row(512), row(512), row(GATE_W)]
    return pl.pallas_call(
        functools.partial(_in_kernel, first=first, nt=nt),
        grid=(t // TILE,),
        in_specs=stream_specs + [
            mod_spec,
            full(g1),
            pl.BlockSpec(w_in_p.shape, lambda i: (0, 0), pipeline_mode=pl.Buffered(1)),
            full(gmat),
            full(gqk),
            pl.BlockSpec((TILE, 256), lambda i: (i % nt, 0)),
            pl.BlockSpec((TILE, 256), lambda i: (i % nt, 0)),
            full(bcg),
        ],
        out_specs=out_specs,
        out_shape=out_shape,
        compiler_params=_cparams(("arbitrary",), 52),
    )(*stream, mods, g1, w_in_p, gmat, gqk, cos_t, sin_t, bcg)


def _attn_a_kernel(q_ref, k_ref, vt_ref, o_ref, s_scr):
    qt = pl.program_id(1)
    left = lax.broadcasted_iota(jnp.int32, (TILE, LANES), 1) < HEAD_DIM

    def query(hd):
        qblk = q_ref[:, (hd // 2) * LANES:(hd // 2 + 1) * LANES]
        zero = jnp.zeros_like(qblk)
        return jnp.where(left, qblk, zero) if hd % 2 == 0 else jnp.where(left, zero, qblk)

    def fold8(a):
        return functools.reduce(jnp.maximum, [a[8 * i:8 * (i + 1), :] for i in range(TILE // 8)])

    def run(n_chunks):
        rows = lambda c: slice(c * TILE, (c + 1) * TILE)

        def scores(hd, qh, c, m8):
            st = _dot_nt(k_ref[rows(c), :], qh)
            s_scr[hd % 2, rows(c), :] = st
            return jnp.maximum(m8, fold8(st))

        def values(hd, c, m, acc):
            p = jnp.exp2(s_scr[hd % 2, rows(c), :] - m).astype(BF16)
            kv = hd % 2
            return acc + _dot(vt_ref[kv * LANES:(kv + 1) * LANES, rows(c)], p)

        neg = jnp.full((8, TILE), NEG_INF, F32)
        qh = query(0)
        m8 = neg
        for c in range(n_chunks):
            m8 = scores(0, qh, c, m8)
        for hd in range(A_HEADS):
            m = jnp.max(m8, axis=0, keepdims=True)
            acc = jnp.zeros((LANES, TILE), F32)
            m8 = neg
            qh = query(hd + 1) if hd + 1 < A_HEADS else None
            for c in range(n_chunks):
                acc = values(hd, c, m, acc)
                if qh is not None:
                    m8 = scores(hd + 1, qh, c, m8)
            o_ref[hd * HEAD_DIM:(hd + 1) * HEAD_DIM, :] = (
                acc[0:HEAD_DIM, :] / acc[HEAD_DIM:HEAD_DIM + 1, :]).astype(BF16)

    @pl.when(qt == 0)
    def _():
        run(1)

    @pl.when(qt > 0)
    def _():
        run(k_ref.shape[0] // TILE)


def _attn_a(qa, ka, vat, nb, nt):
    s = nt * TILE
    w = A_HEADS * HEAD_DIM
    return pl.pallas_call(
        _attn_a_kernel,
        grid=(nb, nt),
        in_specs=[
            pl.BlockSpec((TILE, w), lambda b, i: (b * nt + i, 0)),
            pl.BlockSpec((None, s, LANES), lambda b, i: (b, 0, 0)),
            pl.BlockSpec((None, 2 * LANES, s), lambda b, i: (b, 0, 0)),
        ],
        out_specs=pl.BlockSpec((None, w, TILE), lambda b, i: (b, 0, i)),
        out_shape=jax.ShapeDtypeStruct((nb, w, s), BF16),
        scratch_shapes=[pltpu.VMEM((2, s, TILE), F32)],
        compiler_params=_cparams(("arbitrary",) * 2, 48),
    )(qa, ka.reshape(nb, s, LANES), vat)


NA_V_ROWS = HEAD_DIM + 16


def _attn_b_kernel(q_ref, k0_ref, k1_ref, k2_ref, kc_ref, v0_ref, v1_ref, v2_ref, vc_ref, tab_ref, o_ref, s_scr):
    left = lax.broadcasted_iota(jnp.int32, (TILE, LANES), 1) < HEAD_DIM
    k_refs = (k0_ref, k1_ref, k2_ref, kc_ref)
    v_refs = (v0_ref, v1_ref, v2_ref, vc_ref)
    n_blk = len(k_refs)

    def query(hd):
        qblk = q_ref[:, (hd // 2) * LANES:(hd // 2 + 1) * LANES]
        zero = jnp.zeros_like(qblk)
        return jnp.where(left, qblk, zero) if hd % 2 == 0 else jnp.where(left, zero, qblk)

    def fold8(a):
        return functools.reduce(jnp.maximum, [a[8 * i:8 * (i + 1), :] for i in range(TILE // 8)])

    def scores(hd, qh, j, m8):
        st = _dot_nt(k_refs[j][:, (hd // 2) * LANES:(hd // 2 + 1) * LANES], qh)
        if j < NA_WIN_TILES:
            st = st + tab_ref[hd, j * TILE:(j + 1) * TILE, :]
        s_scr[hd % 2, j] = st
        return jnp.maximum(m8, fold8(st))

    def values(hd, j, m, acc):
        p = jnp.exp2(s_scr[hd % 2, j] - m).astype(BF16)
        return acc + _dot(v_refs[j][hd * NA_V_ROWS:(hd + 1) * NA_V_ROWS, :], p)

    neg = jnp.full((8, TILE), NEG_INF, F32)
    qh = query(0)
    m8 = neg
    for j in range(n_blk):
        m8 = scores(0, qh, j, m8)
    for hd in range(NA_HEADS):
        m = jnp.max(m8, axis=0, keepdims=True)
        acc = jnp.zeros((NA_V_ROWS, TILE), F32)
        m8 = neg
        qh = query(hd + 1) if hd + 1 < NA_HEADS else None
        for j in range(n_blk):
            acc = values(hd, j, m, acc)
            if qh is not None:
                m8 = scores(hd + 1, qh, j, m8)
        o_ref[hd * HEAD_DIM:(hd + 1) * HEAD_DIM, :] = (
            acc[0:HEAD_DIM, :] / acc[HEAD_DIM:HEAD_DIM + 1, :]).astype(BF16)


def _attn_b(qb, kb, vbt, table, nb, nt):
    w = NA_HEADS * HEAD_DIM
    n_lat = nt - 1

    def win(j):
        return lambda i, b: b * nt + 1 + jnp.clip(i - 2, 0, n_lat - NA_WIN_TILES) + j

    ctx = lambda i, b: b * nt
    variant = lambda i, b: (jnp.where(i == 0, 3, jnp.where(i == 1, 0, jnp.where(i == n_lat, 2, 1))), 0, 0, 0)
    kblk = lambda f: pl.BlockSpec((TILE, w), lambda i, b: (f(i, b), 0))
    vblk = lambda f: pl.BlockSpec((None, NA_HEADS * NA_V_ROWS, TILE), lambda i, b: (f(i, b), 0, 0))
    return pl.pallas_call(
        _attn_b_kernel,
        grid=(nt, nb),
        in_specs=[kblk(lambda i, b: b * nt + i),
                  kblk(win(0)), kblk(win(1)), kblk(win(2)), kblk(ctx),
                  vblk(win(0)), vblk(win(1)), vblk(win(2)), vblk(ctx),
                  pl.BlockSpec((None, NA_HEADS, NA_WIN_TILES * TILE, TILE), variant)],
        out_specs=pl.BlockSpec((None, w, TILE), lambda i, b: (b * nt + i, 0, 0)),
        out_shape=jax.ShapeDtypeStruct((nb * nt, w, TILE), BF16),
        scratch_shapes=[pltpu.VMEM((2, NA_WIN_TILES + 1, TILE, TILE), F32)],
        compiler_params=_cparams(("arbitrary", "arbitrary"), 48),
    )(qb, kb, kb, kb, kb, vbt, vbt, vbt, vbt, table)


def _na_table(rpb, rows):
    rq = TILE // GRID_W
    n_lat = rows // rq
    wrows = NA_WIN_TILES * rq
    win_r = min(NA_WIN_R, rows)
    col = np.arange(GRID_W)
    col_start = np.clip(col - NA_WIN_C // 2, 0, GRID_W - NA_WIN_C)
    col_ok = (col[None, :] >= col_start[:, None]) & (col[None, :] < col_start[:, None] + NA_WIN_C)
    assert np.all(np.abs(col[None, :] - col[:, None])[col_ok] < NA_WIN_C)

    depth, nh, n_dr, n_dc = rpb.shape
    lpad = GRID_W - NA_WIN_C
    v = jnp.pad(rpb.astype(F32) * LOG2E, ((0, 0), (0, 0), (0, 0), (lpad, 2 * GRID_W - lpad - n_dc)))
    skew = jnp.tile(v, (1, 1, 1, GRID_W))[..., :GRID_W * (2 * GRID_W - 1)]
    toep = skew.reshape(depth, nh, n_dr, GRID_W, 2 * GRID_W - 1)[..., GRID_W - 1:]
    toep = jnp.pad(toep, ((0, 0), (0, 0), (wrows, wrows), (0, 0), (0, 0)))

    tabs = []
    for tq in (0, 1, n_lat - 1):
        r = tq * rq + np.arange(rq)
        kr = np.clip(tq - 1, 0, n_lat - NA_WIN_TILES) * rq + np.arange(wrows)
        rs = np.clip(r - win_r // 2, 0, rows - win_r)
        row_ok = (kr[None, :] >= rs[:, None]) & (kr[None, :] < rs[:, None] + win_r)
        dr0 = kr[0] - r + NA_WIN_R - 1
        assert np.all(dr0 + wrows >= 0) and np.all(dr0 <= n_dr)
        assert np.all((kr[None, :] - r[:, None] + NA_WIN_R - 1 >= 0)[row_ok])
        assert np.all((kr[None, :] - r[:, None] + NA_WIN_R - 1 < n_dr)[row_ok])
        ok = (row_ok[:, None, :, None] & col_ok[None, :, None, :]).reshape(TILE, wrows * GRID_W)
        blocks = jnp.stack([toep[:, :, a + wrows:a + 2 * wrows] for a in dr0], axis=2)
        bias = blocks.transpose(0, 1, 3, 5, 2, 4).reshape(depth, nh, wrows * GRID_W, TILE)
        tabs.append(jnp.where(ok.T[None, None], bias, NEG_INF))
    tabs.append(jnp.full_like(tabs[0], NEG_INF))
    return jnp.stack(tabs, axis=1)


def _log_sigmoid(x):
    return jnp.minimum(x, 0.0) - jnp.log(1.0 + jnp.exp(-jnp.abs(x)))


def _mlstm_kernel(qf_ref, kf_ref, vf_ref, gf_ref, qb_ref, kb_ref, vb_ref, gb_ref, of_ref, ob_ref, c_s, n_s, m_s):
    step = pl.program_id(1)

    @pl.when(step == 0)
    def _():
        c_s[...] = jnp.zeros_like(c_s)
        n_s[...] = jnp.zeros_like(n_s)
        m_s[...] = jnp.zeros_like(m_s)

    n = TILE
    row = lax.broadcasted_iota(jnp.int32, (n, n), 0)
    col = lax.broadcasted_iota(jnp.int32, (n, n), 1)
    lower, upper = row >= col, row <= col
    dirs = ((qf_ref, kf_ref, vf_ref, gf_ref, of_ref, lower, upper), (qb_ref, kb_ref, vb_ref, gb_ref, ob_ref, upper, lower))
    for direction, (q_ref, k_ref, v_ref, g_ref, o_ref, seen, seen_t) in enumerate(dirs):
        seen_bf = jnp.where(seen, 1.0, 0.0).astype(BF16)
        seen_t_bf = jnp.where(seen_t, 1.0, 0.0).astype(BF16)
        gates = g_ref[...]
        logf = _log_sigmoid(gates)
        gates_t = gates.T
        lf_hi, lf_lo = _split(logf)
        cum_c = _dot(seen_bf, lf_hi) + _dot(seen_bf, lf_lo)
        lft_hi, lft_lo = _split(logf.T)
        cum_r = _dot(lft_hi, seen_t_bf) + _dot(lft_lo, seen_t_bf)
        tot = jnp.sum(logf, axis=0, keepdims=True)

        for hd in range(ML_HEADS):
            ci = 8 * direction + hd
            cf = ci + 4
            st = direction * ML_HEADS + hd
            ig_c = gates[:, ci:ci + 1]
            b_c = cum_c[:, cf:cf + 1]
            ig_r = gates_t[ci:ci + 1, :]
            b_r = cum_r[cf:cf + 1, :]
            b_end = tot[:, cf:cf + 1]
            m_prev = m_s[st, 0:1, 0:1]
            n_prev = n_s[st, 0:1, :]
            c_prev = c_s[st]
            sl = slice(hd * ML_HEAD_DIM, (hd + 1) * ML_HEAD_DIM)
            q = q_ref[:, sl]
            k = k_ref[:, sl]
            v = v_ref[:, sl]

            a = b_c + m_prev
            dmat = jnp.where(seen, b_c - b_r + ig_r, NEG_INF)
            m_t = jnp.maximum(a, jnp.max(dmat, axis=-1, keepdims=True))
            w_inter = jnp.exp(a - m_t)
            smat = _dot_nt(q, k) * jnp.exp(dmat - m_t)
            num = w_inter * _dot(q, c_prev.astype(BF16)) + _dot(smat.astype(BF16), v)
            qn = jnp.sum(q.astype(F32) * n_prev, axis=-1, keepdims=True)
            den = w_inter * qn + jnp.sum(smat, axis=-1, keepdims=True)
            o_ref[:, sl] = (num / jnp.maximum(jnp.abs(den), jnp.exp(-m_t))).astype(BF16)

            log_w = b_end - b_c + ig_c
            m_new = jnp.maximum(b_end + m_prev, jnp.max(log_w, axis=0, keepdims=True))
            g_inter = jnp.exp(b_end + m_prev - m_new)
            kg = k.astype(F32) * jnp.exp(log_w - m_new)
            c_s[st] = g_inter * c_prev + _dot_tn(kg.astype(BF16), v)
            n_s[st] = jnp.broadcast_to(g_inter * n_prev + jnp.sum(kg, axis=0, keepdims=True), n_s.shape[1:])
            m_s[st] = jnp.broadcast_to(m_new, m_s.shape[1:])


def _mlstm(cq, ck, cv, cg, nb, nt):
    t = cq.shape[0]
    w = ML_HEADS * ML_HEAD_DIM
    fwd = lambda b, s: (b * nt + s, 0)
    bwd = lambda b, s: (b * nt + jnp.where(s == 0, 0, nt - s), 0)
    blk = lambda width, f: pl.BlockSpec((TILE, width), f)
    n_state = 2 * ML_HEADS
    return pl.pallas_call(
        _mlstm_kernel,
        grid=(nb, nt),
        in_specs=[blk(w, fwd), blk(w, fwd), blk(w, fwd), blk(GATE_W, fwd),
                  blk(w, bwd), blk(w, bwd), blk(w, bwd), blk(GATE_W, bwd)],
        out_specs=[blk(w, fwd), blk(w, bwd)],
        out_shape=[jax.ShapeDtypeStruct((t, w), BF16)] * 2,
        scratch_shapes=[pltpu.VMEM((n_state, ML_HEAD_DIM, ML_HEAD_DIM), F32),
                        pltpu.VMEM((n_state, 8, ML_HEAD_DIM), F32),
                        pltpu.VMEM((n_state, 8, LANES), F32)],
        compiler_params=_cparams(("arbitrary",) * 2, 32),
    )(cq, ck, cv, cg, cq, ck, cv, cg)


def _merge_kernel(x_ref, ya_ref, yb_ref, hf_ref, hb_ref, co_ref, mod_ref, g1_ref, g2_ref, gml_ref,
                  wmg_ref, bmg_ref, wb_ref, wout_ref, wrh_ref, wrl_ref, brt_ref, xo_ref, h2_ref, dw_ref):
    d = D_MODEL
    mod = mod_ref[...]
    x = x_ref[...]
    h1 = _rms_mod(x, g1_ref[...], mod[:, 0:d], mod[:, d:2 * d]).astype(BF16)

    def gate(br):
        return _sigmoid(_dot(h1, wmg_ref[:, br * d:(br + 1) * d]) + bmg_ref[:, br * d:(br + 1) * d])

    hs = hf_ref[...].astype(F32) + hb_ref[...].astype(F32)
    parts = []
    for hd in range(ML_HEADS):
        v = hs[:, hd * ML_HEAD_DIM:(hd + 1) * ML_HEAD_DIM]
        parts.append(v * lax.rsqrt(jnp.mean(v * v, axis=-1, keepdims=True) + EPS))
    ym = jnp.concatenate(parts, axis=-1) * gml_ref[...] * _sigmoid(co_ref[...].astype(F32))
    merged = gate(0) * _dot_tn(ya_ref[...], wb_ref[0])
    merged = merged + gate(1) * _dot_tn(yb_ref[...], wb_ref[1])
    merged = merged + gate(2) * _dot(ym.astype(BF16), wb_ref[2])
    xn = x + mod[:, 2 * d:3 * d] * _dot(merged.astype(BF16), wout_ref[...])
    xo_ref[...] = xn
    h2 = _rms_mod(xn, g2_ref[...], mod[:, 3 * d:4 * d], mod[:, 4 * d:5 * d])
    h2_ref[...] = h2.astype(BF16)

    h_hi, h_lo = _split(h2)
    logits = _dot(h_hi, wrh_ref[...]) + _dot(h_hi, wrl_ref[...]) + _dot(h_lo, wrh_ref[...]) + brt_ref[...]
    lane = lax.broadcasted_iota(jnp.int32, logits.shape, 1).astype(F32)
    big = 1e9
    is_grp = (lane >= N_EXPERTS) & (lane < N_EXPERTS + N_GROUPS)
    gl = jnp.where(is_grp, logits, NEG_INF)
    gmax = jnp.max(gl, axis=-1, keepdims=True)
    gsel = jnp.min(jnp.where(gl == gmax, lane, big), axis=-1, keepdims=True) - N_EXPERTS
    p_grp = 1.0 / jnp.sum(jnp.exp(gl - gmax), axis=-1, keepdims=True)
    first = gsel * EXPERTS_PER_GROUP
    el = jnp.where((lane >= first) & (lane < first + EXPERTS_PER_GROUP), logits, NEG_INF)
    e1 = jnp.max(el, axis=-1, keepdims=True)
    i1 = jnp.min(jnp.where(el == e1, lane, big), axis=-1, keepdims=True)
    el2 = jnp.where(lane == i1, NEG_INF, el)
    e2 = jnp.max(el2, axis=-1, keepdims=True)
    i2 = jnp.min(jnp.where(el2 == e2, lane, big), axis=-1, keepdims=True)
    r = jnp.exp(e2 - e1)
    w1 = p_grp / (1.0 + r)
    dw_ref[...] = (jnp.where(lane == i1, w1, 0.0) + jnp.where(lane == i2, w1 * r, 0.0)
                   + jnp.where(lane == gsel + N_EXPERTS, 1.0, 0.0))


def _merge(xs, ya, yb, hf, hb, co, mods, g1, g2, gml, wmg, bmg, wb, wout, wrh, wrl, brt, nb, nt):
    t = xs.shape[0]
    d = D_MODEL
    row = lambda w: pl.BlockSpec((TILE, w), lambda i: (i, 0))
    full = lambda a: pl.BlockSpec(a.shape, lambda i: (0,) * a.ndim)
    return pl.pallas_call(
        _merge_kernel,
        grid=(t // TILE,),
        in_specs=[row(d), pl.BlockSpec((None, 512, TILE), lambda i: (i // nt, 0, i % nt)),
                  pl.BlockSpec((None, 512, TILE), lambda i: (i, 0, 0)),
                  row(512), row(512), row(512),
                  pl.BlockSpec((None, 1, 6 * d), _tile_mod_index(nt, nb)),
                  full(g1), full(g2), full(gml), full(wmg), full(bmg), full(wb), full(wout),
                  full(wrh), full(wrl), full(brt)],
        out_specs=[row(d), row(d), row(LANES)],
        out_shape=[jax.ShapeDtypeStruct((t, d), F32), jax.ShapeDtypeStruct((t, d), BF16),
                   jax.ShapeDtypeStruct((t, LANES), F32)],
        compiler_params=_cparams(("arbitrary",), 48),
    )(xs, ya, yb, hf, hb, co, mods, g1, g2, gml, wmg, bmg, wb, wout, wrh, wrl, brt)


MOE_TILE = 1024
MOE_CHUNK = 128


def _moe_kernel(seg_ref, h_ref, dw_ref, wgu_ref, wd_ref, o_ref, hs_s, dws_s, acc_s, pt_s):
    i = pl.program_id(0)
    g = pl.program_id(1)
    tm = h_ref.shape[0]
    ng = N_GROUPS

    @pl.when(g == 0)
    def _():
        dw = dw_ref[...]
        lane = lax.broadcasted_iota(jnp.int32, dw.shape, 1)
        onehot = jnp.where((lane >= N_EXPERTS) & (lane < N_EXPERTS + ng), dw, 0.0)
        onehot_t = onehot.T
        row = lax.broadcasted_iota(jnp.int32, (tm, tm), 0)
        col = lax.broadcasted_iota(jnp.int32, (tm, tm), 1)
        rank_c = _dot(jnp.where(col < row, 1.0, 0.0).astype(BF16), onehot.astype(BF16))
        rank_r = _dot(onehot_t.astype(BF16), jnp.where(row < col, 1.0, 0.0).astype(BF16))
        sub = lax.broadcasted_iota(jnp.int32, onehot_t.shape, 0)
        lo_c = jnp.zeros(dw.shape, F32)
        lo_r = jnp.zeros(onehot_t.shape, F32)
        for k in range(ng):
            lo = seg_ref[(i * ng + k) * 2].astype(F32)
            lo_c = jnp.where(lane == N_EXPERTS + k, lo, lo_c)
            lo_r = jnp.where(sub == N_EXPERTS + k, lo, lo_r)
        pos_c = jnp.sum(onehot * (rank_c + lo_c), axis=1, keepdims=True)
        pos_r = jnp.sum(onehot_t * (rank_r + lo_r), axis=0, keepdims=True)
        perm = jnp.where(pos_r == row.astype(F32), 1.0, 0.0).astype(BF16)
        pt_s[...] = jnp.where(pos_c == col.astype(F32), 1.0, 0.0).astype(BF16)
        hs_s[...] = _dot(perm, h_ref[...]).astype(BF16)
        d1 = dw.astype(BF16)
        r1 = dw - d1.astype(F32)
        d2 = r1.astype(BF16)
        d3 = (r1 - d2.astype(F32)).astype(BF16)
        dws_s[...] = _dot(perm, d1) + _dot(perm, d2) + _dot(perm, d3)
        acc_s[...] = jnp.zeros_like(acc_s)

    lo = seg_ref[(i * ng + g) * 2]
    hi = seg_ref[(i * ng + g) * 2 + 1]
    c_lo = lo // MOE_CHUNK
    c_hi = jnp.where(hi > lo, (hi + MOE_CHUNK - 1) // MOE_CHUNK, c_lo)
    def chunk(c, carry):
        rows = pl.ds(pl.multiple_of(c * MOE_CHUNK, MOE_CHUNK), MOE_CHUNK)
        hc = hs_s[rows, :]
        dwc = dws_s[rows, :]
        lane = lax.broadcasted_iota(jnp.int32, dwc.shape, 1)
        acts = []
        for k in range(EXPERTS_PER_GROUP):
            gu = _dot(hc, wgu_ref[k])
            gate = gu[:, :D_EXPERT]
            wk = jnp.sum(jnp.where(lane == g * EXPERTS_PER_GROUP + k, dwc, 0.0), axis=1, keepdims=True)
            acts.append((gate * _sigmoid(gate) * gu[:, D_EXPERT:] * wk).astype(BF16))
        wd = wd_ref[...].reshape(EXPERTS_PER_GROUP * D_EXPERT, wd_ref.shape[2])
        acc_s[rows, :] += _dot(jnp.concatenate(acts, axis=1), wd)
        return carry

    lax.fori_loop(c_lo, c_hi, chunk, 0)

    @pl.when(g == ng - 1)
    def _():
        o_ref[...] = _dot(pt_s[...], acc_s[...].astype(BF16)).astype(BF16)


def _moe(h2, dw, wgu, wd):
    t, d = h2.shape
    tm = MOE_TILE
    n_tiles = t // tm
    cnt = jnp.sum(dw[:, N_EXPERTS:N_EXPERTS + N_GROUPS].reshape(n_tiles, tm, N_GROUPS), axis=1).astype(jnp.int32)
    hi = jnp.cumsum(cnt, axis=1)
    seg = jnp.stack([hi - cnt, hi], axis=-1).reshape(-1)
    grid_spec = pltpu.PrefetchScalarGridSpec(
        num_scalar_prefetch=1,
        grid=(n_tiles, N_GROUPS),
        in_specs=[pl.BlockSpec((tm, d), lambda i, g, seg: (i, 0)),
                  pl.BlockSpec((tm, LANES), lambda i, g, seg: (i, 0)),
                  pl.BlockSpec((EXPERTS_PER_GROUP, d, 2 * D_EXPERT), lambda i, g, seg: (g, 0, 0)),
                  pl.BlockSpec((EXPERTS_PER_GROUP, D_EXPERT, d), lambda i, g, seg: (g, 0, 0))],
        out_specs=pl.BlockSpec((tm, d), lambda i, g, seg: (i, 0)),
        scratch_shapes=[pltpu.VMEM((tm, d), BF16), pltpu.VMEM((tm, LANES), F32),
                        pltpu.VMEM((tm, d), F32), pltpu.VMEM((tm, tm), BF16)],
    )
    return pl.pallas_call(
        _moe_kernel,
        grid_spec=grid_spec,
        out_shape=jax.ShapeDtypeStruct((t, d), BF16),
        compiler_params=_cparams(("arbitrary", "arbitrary"), 56),
    )(seg, h2, dw, wgu, wd)


def _final_kernel(x_ref, y_ref, mod_ref, o_ref):
    d = D_MODEL
    o_ref[...] = x_ref[...] + mod_ref[:, 5 * d:6 * d] * y_ref[...].astype(F32)


def _final(xn, y, mods, nb, nt):
    d = D_MODEL
    blk = pl.BlockSpec((TILE, d), lambda b, j: (b * nt + 1 + j, 0))
    return pl.pallas_call(
        _final_kernel,
        grid=(nb, nt - 1),
        in_specs=[blk, blk, pl.BlockSpec((None, 1, 6 * d), lambda b, j: (b, 0, 0))],
        out_specs=pl.BlockSpec((None, TILE, d), lambda b, j: (b, j, 0)),
        out_shape=jax.ShapeDtypeStruct((nb, (nt - 1) * TILE, d), F32),
        compiler_params=_cparams(("arbitrary", "arbitrary"), 32),
    )(xn, y, mods)


def _rope_tables(n_tok, n_ctx):
    t = jnp.arange(n_tok)
    rowp = (t // GRID_W).astype(F32)
    colp = (t % GRID_W).astype(F32)
    n_freq = HEAD_DIM // 4
    inv = ROPE_THETA ** (-jnp.arange(n_freq, dtype=F32) / n_freq)
    ang = jnp.concatenate([rowp[:, None] * inv, colp[:, None] * inv], axis=-1)
    cos, sin = jnp.cos(ang), jnp.sin(ang)
    cos64 = jnp.concatenate([cos, cos], axis=-1)
    sin64 = jnp.concatenate([-sin, sin], axis=-1)
    pad = lambda a, v: jnp.concatenate([jnp.full((n_ctx, HEAD_DIM), v, F32), a], axis=0)
    return jnp.tile(pad(cos64, 1.0), (1, 4)), jnp.tile(pad(sin64, 0.0), (1, 4))


def kernel(x, c, ctx, c_ctx, w_mod, b_mod, g_norm, w_in, b_merge, g_qk, rpb, b_mlstm, g_ml, w_branch, w_out,
           w_group, b_group, w_router, b_router, w_gate_up, w_down):
    nb, n_tok, d = x.shape
    n_ctx = ctx.shape[1]
    depth = w_mod.shape[0]
    assert d == D_MODEL and n_ctx == TILE and n_tok % TILE == 0
    rows = n_tok // GRID_W
    nt = (n_ctx + n_tok) // TILE
    assert nt - 1 >= NA_WIN_TILES and rows >= NA_WIN_R
    t = nb * nt * TILE

    n_mod = -(-(nb + 1) // 8) * 8
    cc = jnp.zeros((n_mod, d), F32).at[:nb].set(c).at[nb].set(c_ctx)
    mods_all = _modulation(cc, w_mod, b_mod)

    cos_t, sin_t = _rope_tables(n_tok, n_ctx)
    gmat = jnp.asarray(np.kron(np.eye(256 // HEAD_DIM), np.ones((HEAD_DIM, HEAD_DIM))), BF16)
    a_cols = np.concatenate([np.arange(h * HEAD_DIM, (h + 1) * HEAD_DIM) for h in _A_HEAD_ORDER])
    na_tables = _na_table(rpb, rows)

    assert t % MOE_TILE == 0
    stream = None
    for l in range(depth):
        mods = mods_all[l].reshape(n_mod, 1, 6 * d)
        wl = w_in[l]
        w_in_p = jnp.concatenate(
            [wl[:, a_cols], wl[:, 512:4352], jnp.pad(wl[:, 4352:4368], ((0, 0), (0, GATE_W - 16)))],
            axis=1).astype(BF16)
        assert w_in_p.shape[1] == IN_WIDTH_P
        wmg = wl[:, 4368:].astype(BF16)
        gqk = jnp.pad(jnp.tile(g_qk[l], (1, 256 // HEAD_DIM)), ((0, 4), (0, 0)))
        bcg = jnp.pad(b_mlstm[l].reshape(1, 16), ((0, 0), (0, GATE_W - 16)))
        bmg = b_merge[l].reshape(1, N_BRANCH * d)
        first = stream is None
        xs, qa, ka, vat, qb, kb, vbt, cq, ck, cv, co, cg = _in_projection(
            (x, ctx, mods) if first else stream, first, mods, g_norm[l, 0:1], w_in_p, gmat, gqk, cos_t, sin_t,
            bcg, nb, nt)

        ya = _attn_a(qa, ka, vat, nb, nt)
        yb = _attn_b(qb, kb, vbt, na_tables[l], nb, nt)
        hf, hb = _mlstm(cq, ck, cv, cg, nb, nt)

        wb = jnp.stack([w_branch[l, 0][a_cols], w_branch[l, 1], w_branch[l, 2]]).astype(BF16)
        w_rt = jnp.concatenate([w_router[l], w_group[l], jnp.zeros((d, LANES - N_EXPERTS - N_GROUPS), F32)], axis=1)
        wrh = w_rt.astype(BF16)
        wrl = (w_rt - wrh.astype(F32)).astype(BF16)
        brt = jnp.concatenate([b_router[l], b_group[l], jnp.zeros((LANES - N_EXPERTS - N_GROUPS,), F32)])[None, :]
        xs, h2, dw = _merge(xs, ya, yb, hf, hb, co, mods, g_norm[l, 0:1], g_norm[l, 1:2], g_ml[l][None, :],
                            wmg, bmg, wb, w_out[l].astype(BF16), wrh, wrl, brt, nb, nt)
        stream = (xs, _moe(h2, dw, w_gate_up[l].astype(BF16), w_down[l].astype(BF16)), mods)

    return _final(*stream, nb, nt)
```

```python
import functools

import numpy as np
import jax
import jax.numpy as jnp
from jax import lax
from jax.experimental import pallas as pl
from jax.experimental.pallas import tpu as pltpu

D_MODEL = 1024
GRID_W = 64
HEAD_DIM = 64
ROPE_THETA = 10000.0
A_HEADS = 8
A_KV_HEADS = 2
NA_HEADS = 8
NA_WIN_R = 8
NA_WIN_C = 16
ML_HEADS = 4
ML_HEAD_DIM = 128
N_BRANCH = 3
BRANCH_W = 512
N_GROUPS = 4
EXPERTS_PER_GROUP = 8
N_EXPERTS = N_GROUPS * EXPERTS_PER_GROUP
D_EXPERT = 256
EPS = 1e-6
NEG_INF = -1e30
LOG2E = 1.4426950408889634

TILE = 256
LANES = 128
NA_WIN_TILES = 3
GATE_W = 128
V7X_VMEM_BYTES = 64 * 1024 * 1024

F32 = jnp.float32
BF16 = jnp.bfloat16

_O_QA, _O_KA, _O_VA = 0, 512, 640
_O_QB, _O_KB, _O_VB = 768, 1280, 1792
_O_CQ, _O_CK, _O_CV, _O_CO = 2304, 2816, 3328, 3840
_O_CG = 4352
IN_WIDTH_P = _O_CG + GATE_W
_A_HEAD_ORDER = (0, 4, 1, 5, 2, 6, 3, 7)


def _cparams(sem, vmem_mb):
    return pltpu.CompilerParams(dimension_semantics=sem, vmem_limit_bytes=vmem_mb * 1024 * 1024)


def _dot(a, b):
    return jnp.dot(a, b, preferred_element_type=F32)


def _dot_nt(a, b):
    return lax.dot_general(a, b, (((1,), (1,)), ((), ())), preferred_element_type=F32)


def _dot_tn(a, b):
    return lax.dot_general(a, b, (((0,), (0,)), ((), ())), preferred_element_type=F32)


def _split(a):
    hi = a.astype(BF16)
    lo = (a - hi.astype(F32)).astype(BF16)
    return hi, lo


def _sigmoid(x):
    return 0.5 * jnp.tanh(0.5 * x) + 0.5


def _rms_mod(x, g, shift, scale):
    ms = jnp.mean(x * x, axis=-1, keepdims=True)
    y = x * lax.rsqrt(ms + EPS) * g
    return y * (1.0 + scale) + shift


def _mod_kernel(c_ref, w_ref, b_ref, o_ref):
    c = c_ref[...]
    a_hi, a_lo = _split(c * _sigmoid(c))
    w_hi, w_lo = _split(w_ref[...])
    o_ref[...] = _dot(a_hi, w_hi) + _dot(a_hi, w_lo) + _dot(a_lo, w_hi) + b_ref[...]


def _modulation(cc, w_mod, b_mod):
    depth, d, n = w_mod.shape
    r = cc.shape[0]
    bn = 1024
    return pl.pallas_call(
        _mod_kernel,
        grid=(depth, n // bn),
        in_specs=[
            pl.BlockSpec((r, d), lambda l, j: (0, 0)),
            pl.BlockSpec((None, d, bn), lambda l, j: (l, 0, j)),
            pl.BlockSpec((None, 1, bn), lambda l, j: (l, 0, j)),
        ],
        out_specs=pl.BlockSpec((None, r, bn), lambda l, j: (l, 0, j)),
        out_shape=jax.ShapeDtypeStruct((depth, r, n), F32),
        compiler_params=_cparams(("arbitrary", "arbitrary"), 32),
    )(cc, w_mod, b_mod.reshape(depth, 1, n))


def _group_norm64(acc, gmat, gain):
    ss = _dot((acc * acc).astype(BF16), gmat)
    return acc * lax.rsqrt(ss * (1.0 / HEAD_DIM) + EPS) * gain


def _rope(x, cos, sin_signed):
    w = x.shape[1]
    lane = lax.broadcasted_iota(jnp.int32, x.shape, 1)
    nxt = pltpu.roll(x, w - HEAD_DIM // 2, 1)
    prv = pltpu.roll(x, HEAD_DIM // 2, 1)
    rot = jnp.where((lane % HEAD_DIM) < HEAD_DIM // 2, nxt, prv)
    return x * cos + rot * sin_signed


def _in_kernel(a_ref, b_ref, c_ref, mod_ref, g_ref, w_ref, gmat_ref, gqk_ref, cos_ref, sin_ref, bcg_ref,
               xo_ref, qa_ref, ka_ref, vat_ref, qb_ref, kb_ref, vbt_ref, cq_ref, ck_ref, cv_ref, co_ref, cg_ref,
               *, first, nt):
    d = D_MODEL
    if first:
        x = jnp.where(pl.program_id(0) % nt == 0, b_ref[...], a_ref[...])
    else:
        x = a_ref[...] + c_ref[:, 5 * d:6 * d] * b_ref[...].astype(F32)
    xo_ref[...] = x
    mod = mod_ref[...]
    h = _rms_mod(x, g_ref[...], mod[:, 0:d], mod[:, d:2 * d]).astype(BF16)

    def proj(a, width):
        return _dot(h, w_ref[:, a:a + width])

    gmat = gmat_ref[...]
    cos = cos_ref[...]
    sin = sin_ref[...]
    att_scale = HEAD_DIM ** -0.5
    pa = proj(_O_QA, _O_QB - _O_QA)
    for s in range(2):
        acc = _group_norm64(pa[:, 256 * s:256 * (s + 1)], gmat, gqk_ref[0:1, :])
        qa_ref[:, 256 * s:256 * (s + 1)] = (_rope(acc, cos, sin) * (att_scale * LOG2E)).astype(BF16)
    acc = _group_norm64(pa[:, _O_KA:_O_VA], gmat[:128, :128], gqk_ref[1:2, :128])
    ka_ref[...] = _rope(acc, cos[:, :128], sin[:, :128]).astype(BF16)
    vt = pa[:, _O_VA:_O_QB].T
    ones = jnp.ones((HEAD_DIM, TILE), F32)
    vat_ref[...] = jnp.concatenate([vt[:HEAD_DIM], ones, vt[HEAD_DIM:], ones], axis=0).astype(BF16)
    pb = proj(_O_QB, _O_VB - _O_QB)
    for s in range(2):
        acc = _group_norm64(pb[:, 256 * s:256 * (s + 1)], gmat, gqk_ref[2:3, :])
        qb_ref[:, 256 * s:256 * (s + 1)] = (acc * (att_scale * LOG2E)).astype(BF16)
        acc = _group_norm64(pb[:, 512 + 256 * s:512 + 256 * (s + 1)], gmat, gqk_ref[3:4, :])
        kb_ref[:, 256 * s:256 * (s + 1)] = acc.astype(BF16)
    vbt = proj(_O_VB, 512).T
    ones = jnp.ones((NA_V_ROWS - HEAD_DIM, TILE), F32)
    vbt_ref[...] = jnp.concatenate(
        [piece for hd in range(NA_HEADS) for piece in (vbt[hd * HEAD_DIM:(hd + 1) * HEAD_DIM], ones)],
        axis=0).astype(BF16)
    pc = proj(_O_CQ, IN_WIDTH_P - _O_CQ)
    cq_ref[...] = pc[:, 0:512].astype(BF16)
    ck_ref[...] = (pc[:, 512:1024] * (ML_HEAD_DIM ** -0.5)).astype(BF16)
    cv_ref[...] = pc[:, 1024:1536].astype(BF16)
    co_ref[...] = pc[:, 1536:2048].astype(BF16)
    cg_ref[...] = pc[:, 2048:2048 + GATE_W] + bcg_ref[...]


def _tile_mod_index(nt, nb):
    return lambda i: (jnp.where(i % nt == 0, nb, i // nt), 0, 0)


def _in_projection(stream, first, mods, g1, w_in_p, gmat, gqk, cos_t, sin_t, bcg, nb, nt):
    t = nb * nt * TILE
    d = D_MODEL
    row = lambda w: pl.BlockSpec((TILE, w), lambda i: (i, 0))
    full = lambda a: pl.BlockSpec(a.shape, lambda i: (0,) * a.ndim)
    mod_spec = pl.BlockSpec((None, 1, 6 * d), _tile_mod_index(nt, nb))
    if first:
        stream_specs = [pl.BlockSpec((None, TILE, d), lambda i: (i // nt, jnp.maximum(i % nt - 1, 0), 0)),
                        pl.BlockSpec((None, TILE, d), lambda i: (i // nt, 0, 0)), mod_spec]
    else:
        stream_specs = [row(d), row(d), mod_spec]
    act = jax.ShapeDtypeStruct((t, 512), BF16)
    vbt_rows = NA_HEADS * NA_V_ROWS
    out_shape = [jax.ShapeDtypeStruct((t, d), F32), act, jax.ShapeDtypeStruct((t, LANES), BF16),
                 jax.ShapeDtypeStruct((nb, 2 * LANES, nt * TILE), BF16),
                 act, act, jax.ShapeDtypeStruct((t // TILE, vbt_rows, TILE), BF16), act, act, act, act,
                 jax.ShapeDtypeStruct((t, GATE_W), F32)]
    out_specs = [row(d), row(512), row(LANES), pl.BlockSpec((None, 2 * LANES, TILE), lambda i: (i // nt, 0, i % nt)),
                 row(512), row(512), pl.BlockSpec((None, vbt_rows, TILE), lambda i: (i, 0, 0)),
                 row(512), row(512), row(512), row(512), row(GATE_W)]
    return pl.pallas_call(
        functools.partial(_in_kernel, first=first, nt=nt),
        grid=(t // TILE,),
        in_specs=stream_specs + [
            mod_spec,
            full(g1),
            pl.BlockSpec(w_in_p.shape, lambda i: (0, 0), pipeline_mode=pl.Buffered(1)),
            full(gmat),
            full(gqk),
            pl.BlockSpec((TILE, 256), lambda i: (i % nt, 0)),
            pl.BlockSpec((TILE, 256), lambda i: (i % nt, 0)),
            full(bcg),
        ],
        out_specs=out_specs,
        out_shape=out_shape,
        compiler_params=_cparams(("arbitrary",), 52),
    )(*stream, mods, g1, w_in_p, gmat, gqk, cos_t, sin_t, bcg)


def _attn_a_kernel(q_ref, k_ref, vt_ref, o_ref, s_scr):
    qt = pl.program_id(1)
    left = lax.broadcasted_iota(jnp.int32, (TILE, LANES), 1) < HEAD_DIM

    def query(hd):
        qblk = q_ref[:, (hd // 2) * LANES:(hd // 2 + 1) * LANES]
        zero = jnp.zeros_like(qblk)
        return jnp.where(left, qblk, zero) if hd % 2 == 0 else jnp.where(left, zero, qblk)

    def fold8(a):
        return functools.reduce(jnp.maximum, [a[8 * i:8 * (i + 1), :] for i in range(TILE // 8)])

    def run(n_chunks):
        rows = lambda c: slice(c * TILE, (c + 1) * TILE)

        def scores(hd, qh, c, m8):
            st = _dot_nt(k_ref[rows(c), :], qh)
            s_scr[hd % 2, rows(c), :] = st
            return jnp.maximum(m8, fold8(st))

        def values(hd, c, m, acc):
            p = jnp.exp2(s_scr[hd % 2, rows(c), :] - m).astype(BF16)
            kv = hd % 2
            return acc + _dot(vt_ref[kv * LANES:(kv + 1) * LANES, rows(c)], p)

        neg = jnp.full((8, TILE), NEG_INF, F32)
        qh = query(0)
        m8 = neg
        for c in range(n_chunks):
            m8 = scores(0, qh, c, m8)
        for hd in range(A_HEADS):
            m = jnp.max(m8, axis=0, keepdims=True)
            acc = jnp.zeros((LANES, TILE), F32)
            m8 = neg
            qh = query(hd + 1) if hd + 1 < A_HEADS else None
            for c in range(n_chunks):
                acc = values(hd, c, m, acc)
                if qh is not None:
                    m8 = scores(hd + 1, qh, c, m8)
            o_ref[hd * HEAD_DIM:(hd + 1) * HEAD_DIM, :] = (
                acc[0:HEAD_DIM, :] / acc[HEAD_DIM:HEAD_DIM + 1, :]).astype(BF16)

    @pl.when(qt == 0)
    def _():
        run(1)

    @pl.when(qt > 0)
    def _():
        run(k_ref.shape[0] // TILE)


def _attn_a(qa, ka, vat, nb, nt):
    s = nt * TILE
    w = A_HEADS * HEAD_DIM
    return pl.pallas_call(
        _attn_a_kernel,
        grid=(nb, nt),
        in_specs=[
            pl.BlockSpec((TILE, w), lambda b, i: (b * nt + i, 0)),
            pl.BlockSpec((None, s, LANES), lambda b, i: (b, 0, 0)),
            pl.BlockSpec((None, 2 * LANES, s), lambda b, i: (b, 0, 0)),
        ],
        out_specs=pl.BlockSpec((None, w, TILE), lambda b, i: (b, 0, i)),
        out_shape=jax.ShapeDtypeStruct((nb, w, s), BF16),
        scratch_shapes=[pltpu.VMEM((2, s, TILE), F32)],
        compiler_params=_cparams(("arbitrary",) * 2, 48),
    )(qa, ka.reshape(nb, s, LANES), vat)


NA_V_ROWS = HEAD_DIM + 16


def _attn_b_kernel(q_ref, k0_ref, k1_ref, k2_ref, kc_ref, v0_ref, v1_ref, v2_ref, vc_ref, tab_ref, o_ref, s_scr):
    left = lax.broadcasted_iota(jnp.int32, (TILE, LANES), 1) < HEAD_DIM
    k_refs = (k0_ref, k1_ref, k2_ref, kc_ref)
    v_refs = (v0_ref, v1_ref, v2_ref, vc_ref)
    n_blk = len(k_refs)

    def query(hd):
        qblk = q_ref[:, (hd // 2) * LANES:(hd // 2 + 1) * LANES]
        zero = jnp.zeros_like(qblk)
        return jnp.where(left, qblk, zero) if hd % 2 == 0 else jnp.where(left, zero, qblk)

    def fold8(a):
        return functools.reduce(jnp.maximum, [a[8 * i:8 * (i + 1), :] for i in range(TILE // 8)])

    def scores(hd, qh, j, m8):
        st = _dot_nt(k_refs[j][:, (hd // 2) * LANES:(hd // 2 + 1) * LANES], qh)
        if j < NA_WIN_TILES:
            st = st + tab_ref[hd, j * TILE:(j + 1) * TILE, :]
        s_scr[hd % 2, j] = st
        return jnp.maximum(m8, fold8(st))

    def values(hd, j, m, acc):
        p = jnp.exp2(s_scr[hd % 2, j] - m).astype(BF16)
        return acc + _dot(v_refs[j][hd * NA_V_ROWS:(hd + 1) * NA_V_ROWS, :], p)

    neg = jnp.full((8, TILE), NEG_INF, F32)
    qh = query(0)
    m8 = neg
    for j in range(n_blk):
        m8 = scores(0, qh, j, m8)
    for hd in range(NA_HEADS):
        m = jnp.max(m8, axis=0, keepdims=True)
        acc = jnp.zeros((NA_V_ROWS, TILE), F32)
        m8 = neg
        qh = query(hd + 1) if hd + 1 < NA_HEADS else None
        for j in range(n_blk):
            acc = values(hd, j, m, acc)
            if qh is not None:
                m8 = scores(hd + 1, qh, j, m8)
        o_ref[hd * HEAD_DIM:(hd + 1) * HEAD_DIM, :] = (
            acc[0:HEAD_DIM, :] / acc[HEAD_DIM:HEAD_DIM + 1, :]).astype(BF16)


def _attn_b(qb, kb, vbt, table, nb, nt):
    w = NA_HEADS * HEAD_DIM
    n_lat = nt - 1

    def win(j):
        return lambda i, b: b * nt + 1 + jnp.clip(i - 2, 0, n_lat - NA_WIN_TILES) + j

    ctx = lambda i, b: b * nt
    variant = lambda i, b: (jnp.where(i == 0, 3, jnp.where(i == 1, 0, jnp.where(i == n_lat, 2, 1))), 0, 0, 0)
    kblk = lambda f: pl.BlockSpec((TILE, w), lambda i, b: (f(i, b), 0))
    vblk = lambda f: pl.BlockSpec((None, NA_HEADS * NA_V_ROWS, TILE), lambda i, b: (f(i, b), 0, 0))
    return pl.pallas_call(
        _attn_b_kernel,
        grid=(nt, nb),
        in_specs=[kblk(lambda i, b: b * nt + i),
                  kblk(win(0)), kblk(win(1)), kblk(win(2)), kblk(ctx),
                  vblk(win(0)), vblk(win(1)), vblk(win(2)), vblk(ctx),
                  pl.BlockSpec((None, NA_HEADS, NA_WIN_TILES * TILE, TILE), variant)],
        out_specs=pl.BlockSpec((None, w, TILE), lambda i, b: (b * nt + i, 0, 0)),
        out_shape=jax.ShapeDtypeStruct((nb * nt, w, TILE), BF16),
        scratch_shapes=[pltpu.VMEM((2, NA_WIN_TILES + 1, TILE, TILE), F32)],
        compiler_params=_cparams(("arbitrary", "arbitrary"), 48),
    )(qb, kb, kb, kb, kb, vbt, vbt, vbt, vbt, table)


def _na_table(rpb, rows):
    rq = TILE // GRID_W
    n_lat = rows // rq
    wrows = NA_WIN_TILES * rq
    win_r = min(NA_WIN_R, rows)
    col = np.arange(GRID_W)
    col_start = np.clip(col - NA_WIN_C // 2, 0, GRID_W - NA_WIN_C)
    col_ok = (col[None, :] >= col_start[:, None]) & (col[None, :] < col_start[:, None] + NA_WIN_C)
    assert np.all(np.abs(col[None, :] - col[:, None])[col_ok] < NA_WIN_C)

    depth, nh, n_dr, n_dc = rpb.shape
    lpad = GRID_W - NA_WIN_C
    v = jnp.pad(rpb.astype(F32) * LOG2E, ((0, 0), (0, 0), (0, 0), (lpad, 2 * GRID_W - lpad - n_dc)))
    skew = jnp.tile(v, (1, 1, 1, GRID_W))[..., :GRID_W * (2 * GRID_W - 1)]
    toep = skew.reshape(depth, nh, n_dr, GRID_W, 2 * GRID_W - 1)[..., GRID_W - 1:]
    toep = jnp.pad(toep, ((0, 0), (0, 0), (wrows, wrows), (0, 0), (0, 0)))

    tabs = []
    for tq in (0, 1, n_lat - 1):
        r = tq * rq + np.arange(rq)
        kr = np.clip(tq - 1, 0, n_lat - NA_WIN_TILES) * rq + np.arange(wrows)
        rs = np.clip(r - win_r // 2, 0, rows - win_r)
        row_ok = (kr[None, :] >= rs[:, None]) & (kr[None, :] < rs[:, None] + win_r)
        dr0 = kr[0] - r + NA_WIN_R - 1
        assert np.all(dr0 + wrows >= 0) and np.all(dr0 <= n_dr)
        assert np.all((kr[None, :] - r[:, None] + NA_WIN_R - 1 >= 0)[row_ok])
        assert np.all((kr[None, :] - r[:, None] + NA_WIN_R - 1 < n_dr)[row_ok])
        ok = (row_ok[:, None, :, None] & col_ok[None, :, None, :]).reshape(TILE, wrows * GRID_W)
        blocks = jnp.stack([toep[:, :, a + wrows:a + 2 * wrows] for a in dr0], axis=2)
        bias = blocks.transpose(0, 1, 3, 5, 2, 4).reshape(depth, nh, wrows * GRID_W, TILE)
        tabs.append(jnp.where(ok.T[None, None], bias, NEG_INF))
    tabs.append(jnp.full_like(tabs[0], NEG_INF))
    return jnp.stack(tabs, axis=1)


def _log_sigmoid(x):
    return jnp.minimum(x, 0.0) - jnp.log(1.0 + jnp.exp(-jnp.abs(x)))


def _mlstm_kernel(qf_ref, kf_ref, vf_ref, gf_ref, qb_ref, kb_ref, vb_ref, gb_ref, of_ref, ob_ref, c_s, n_s, m_s):
    step = pl.program_id(1)

    @pl.when(step == 0)
    def _():
        c_s[...] = jnp.zeros_like(c_s)
        n_s[...] = jnp.zeros_like(n_s)
        m_s[...] = jnp.zeros_like(m_s)

    n = TILE
    row = lax.broadcasted_iota(jnp.int32, (n, n), 0)
    col = lax.broadcasted_iota(jnp.int32, (n, n), 1)
    lower, upper = row >= col, row <= col
    dirs = ((qf_ref, kf_ref, vf_ref, gf_ref, of_ref, lower, upper), (qb_ref, kb_ref, vb_ref, gb_ref, ob_ref, upper, lower))
    for direction, (q_ref, k_ref, v_ref, g_ref, o_ref, seen, seen_t) in enumerate(dirs):
        seen_bf = jnp.where(seen, 1.0, 0.0).astype(BF16)
        seen_t_bf = jnp.where(seen_t, 1.0, 0.0).astype(BF16)
        gates = g_ref[...]
        logf = _log_sigmoid(gates)
        gates_t = gates.T
        lf_hi, lf_lo = _split(logf)
        cum_c = _dot(seen_bf, lf_hi) + _dot(seen_bf, lf_lo)
        lft_hi, lft_lo = _split(logf.T)
        cum_r = _dot(lft_hi, seen_t_bf) + _dot(lft_lo, seen_t_bf)
        tot = jnp.sum(logf, axis=0, keepdims=True)

        for hd in range(ML_HEADS):
            ci = 8 * direction + hd
            cf = ci + 4
            st = direction * ML_HEADS + hd
            ig_c = gates[:, ci:ci + 1]
            b_c = cum_c[:, cf:cf + 1]
            ig_r = gates_t[ci:ci + 1, :]
            b_r = cum_r[cf:cf + 1, :]
            b_end = tot[:, cf:cf + 1]
            m_prev = m_s[st, 0:1, 0:1]
            n_prev = n_s[st, 0:1, :]
            c_prev = c_s[st]
            sl = slice(hd * ML_HEAD_DIM, (hd + 1) * ML_HEAD_DIM)
            q = q_ref[:, sl]
            k = k_ref[:, sl]
            v = v_ref[:, sl]

            a = b_c + m_prev
            dmat = jnp.where(seen, b_c - b_r + ig_r, NEG_INF)
            m_t = jnp.maximum(a, jnp.max(dmat, axis=-1, keepdims=True))
            w_inter = jnp.exp(a - m_t)
            smat = _dot_nt(q, k) * jnp.exp(dmat - m_t)
            num = w_inter * _dot(q, c_prev.astype(BF16)) + _dot(smat.astype(BF16), v)
            qn = jnp.sum(q.astype(F32) * n_prev, axis=-1, keepdims=True)
            den = w_inter * qn + jnp.sum(smat, axis=-1, keepdims=True)
            o_ref[:, sl] = (num / jnp.maximum(jnp.abs(den), jnp.exp(-m_t))).astype(BF16)

            log_w = b_end - b_c + ig_c
            m_new = jnp.maximum(b_end + m_prev, jnp.max(log_w, axis=0, keepdims=True))
            g_inter = jnp.exp(b_end + m_prev - m_new)
            kg = k.astype(F32) * jnp.exp(log_w - m_new)
            c_s[st] = g_inter * c_prev + _dot_tn(kg.astype(BF16), v)
            n_s[st] = jnp.broadcast_to(g_inter * n_prev + jnp.sum(kg, axis=0, keepdims=True), n_s.shape[1:])
            m_s[st] = jnp.broadcast_to(m_new, m_s.shape[1:])


def _mlstm(cq, ck, cv, cg, nb, nt):
    t = cq.shape[0]
    w = ML_HEADS * ML_HEAD_DIM
    fwd = lambda b, s: (b * nt + s, 0)
    bwd = lambda b, s: (b * nt + jnp.where(s == 0, 0, nt - s), 0)
    blk = lambda width, f: pl.BlockSpec((TILE, width), f)
    n_state = 2 * ML_HEADS
    return pl.pallas_call(
        _mlstm_kernel,
        grid=(nb, nt),
        in_specs=[blk(w, fwd), blk(w, fwd), blk(w, fwd), blk(GATE_W, fwd),
                  blk(w, bwd), blk(w, bwd), blk(w, bwd), blk(GATE_W, bwd)],
        out_specs=[blk(w, fwd), blk(w, bwd)],
        out_shape=[jax.ShapeDtypeStruct((t, w), BF16)] * 2,
        scratch_shapes=[pltpu.VMEM((n_state, ML_HEAD_DIM, ML_HEAD_DIM), F32),
                        pltpu.VMEM((n_state, 8, ML_HEAD_DIM), F32),
                        pltpu.VMEM((n_state, 8, LANES), F32)],
        compiler_params=_cparams(("arbitrary",) * 2, 32),
    )(cq, ck, cv, cg, cq, ck, cv, cg)


def _merge_kernel(x_ref, ya_ref, yb_ref, hf_ref, hb_ref, co_ref, mod_ref, g1_ref, g2_ref, gml_ref,
                  wmg_ref, bmg_ref, wb_ref, wout_ref, wr3_ref, brt_ref, xo_ref, h2_ref, dw_ref):
    d = D_MODEL
    mod = mod_ref[...]
    x = x_ref[...]
    h1 = _rms_mod(x, g1_ref[...], mod[:, 0:d], mod[:, d:2 * d]).astype(BF16)

    def gate(br):
        return _sigmoid(_dot(h1, wmg_ref[:, br * d:(br + 1) * d]) + bmg_ref[:, br * d:(br + 1) * d])

    hs = hf_ref[...].astype(F32) + hb_ref[...].astype(F32)
    parts = []
    for hd in range(ML_HEADS):
        v = hs[:, hd * ML_HEAD_DIM:(hd + 1) * ML_HEAD_DIM]
        parts.append(v * lax.rsqrt(jnp.mean(v * v, axis=-1, keepdims=True) + EPS))
    ym = jnp.concatenate(parts, axis=-1) * gml_ref[...] * _sigmoid(co_ref[...].astype(F32))
    merged = gate(0) * _dot_tn(ya_ref[...], wb_ref[0])
    merged = merged + gate(1) * _dot_tn(yb_ref[...], wb_ref[1])
    merged = merged + gate(2) * _dot(ym.astype(BF16), wb_ref[2])
    xn = x + mod[:, 2 * d:3 * d] * _dot(merged.astype(BF16), wout_ref[...])
    xo_ref[...] = xn
    h2 = _rms_mod(xn, g2_ref[...], mod[:, 3 * d:4 * d], mod[:, 4 * d:5 * d])
    h2_ref[...] = h2.astype(BF16)

    h_hi, h_lo = _split(h2)
    logits = _dot(jnp.concatenate([h_hi, h_hi, h_lo], axis=1), wr3_ref[...]) + brt_ref[...]
    lane = lax.broadcasted_iota(jnp.int32, logits.shape, 1).astype(F32)
    big = 1e9
    is_grp = (lane >= N_EXPERTS) & (lane < N_EXPERTS + N_GROUPS)
    gl = jnp.where(is_grp, logits, NEG_INF)
    gmax = jnp.max(gl, axis=-1, keepdims=True)
    gsel = jnp.min(jnp.where(gl == gmax, lane, big), axis=-1, keepdims=True) - N_EXPERTS
    p_grp = 1.0 / jnp.sum(jnp.exp(gl - gmax), axis=-1, keepdims=True)
    first = gsel * EXPERTS_PER_GROUP
    el = jnp.where((lane >= first) & (lane < first + EXPERTS_PER_GROUP), logits, NEG_INF)
    e1 = jnp.max(el, axis=-1, keepdims=True)
    i1 = jnp.min(jnp.where(el == e1, lane, big), axis=-1, keepdims=True)
    el2 = jnp.where(lane == i1, NEG_INF, el)
    e2 = jnp.max(el2, axis=-1, keepdims=True)
    i2 = jnp.min(jnp.where(el2 == e2, lane, big), axis=-1, keepdims=True)
    r = jnp.exp(e2 - e1)
    w1 = p_grp / (1.0 + r)
    dw_ref[...] = (jnp.where(lane == i1, w1, 0.0) + jnp.where(lane == i2, w1 * r, 0.0)
                   + jnp.where(lane == gsel + N_EXPERTS, 1.0, 0.0))


def _merge(xs, ya, yb, hf, hb, co, mods, g1, g2, gml, wmg, bmg, wb, wout, wr3, brt, nb, nt):
    t = xs.shape[0]
    d = D_MODEL
    row = lambda w: pl.BlockSpec((TILE, w), lambda i: (i, 0))
    full = lambda a: pl.BlockSpec(a.shape, lambda i: (0,) * a.ndim)
    return pl.pallas_call(
        _merge_kernel,
        grid=(t // TILE,),
        in_specs=[row(d), pl.BlockSpec((None, 512, TILE), lambda i: (i // nt, 0, i % nt)),
                  pl.BlockSpec((None, 512, TILE), lambda i: (i, 0, 0)),
                  row(512), row(512), row(512),
                  pl.BlockSpec((None, 1, 6 * d), _tile_mod_index(nt, nb)),
                  full(g1), full(g2), full(gml), full(wmg), full(bmg), full(wb), full(wout),
                  full(wr3), full(brt)],
        out_specs=[row(d), row(d), row(LANES)],
        out_shape=[jax.ShapeDtypeStruct((t, d), F32), jax.ShapeDtypeStruct((t, d), BF16),
                   jax.ShapeDtypeStruct((t, LANES), F32)],
        compiler_params=_cparams(("arbitrary",), 48),
    )(xs, ya, yb, hf, hb, co, mods, g1, g2, gml, wmg, bmg, wb, wout, wr3, brt)


MOE_TILE = 1024
MOE_CHUNK = 128


MOE_TILES_PER_STEP = 2
MOE_EXPERTS_PER_STEP = 4


def _moe_kernel(seg_ref, h_ref, dw_ref, wgu_ref, wd_ref, o_ref, hs_s, dws_s, acc_s, pos_s):
    pair = pl.program_id(0)
    s = pl.program_id(1)
    n_steps = N_EXPERTS // MOE_EXPERTS_PER_STEP
    tm = MOE_TILE
    ng = N_GROUPS
    g = s // (EXPERTS_PER_GROUP // MOE_EXPERTS_PER_STEP)

    @pl.when(s == 0)
    def _():
        row = lax.broadcasted_iota(jnp.int32, (tm, tm), 0)
        col = lax.broadcasted_iota(jnp.int32, (tm, tm), 1)
        before = jnp.where(col < row, 1.0, 0.0).astype(BF16)
        after = jnp.where(row < col, 1.0, 0.0).astype(BF16)
        for k in range(MOE_TILES_PER_STEP):
            rows = slice(k * tm, (k + 1) * tm)
            tile = pair * MOE_TILES_PER_STEP + k
            dw = dw_ref[rows, :]
            lane = lax.broadcasted_iota(jnp.int32, dw.shape, 1)
            onehot = jnp.where((lane >= N_EXPERTS) & (lane < N_EXPERTS + ng), dw, 0.0)
            onehot_t = onehot.T
            rank_c = _dot(before, onehot.astype(BF16))
            rank_r = _dot(onehot_t.astype(BF16), after)
            sub = lax.broadcasted_iota(jnp.int32, onehot_t.shape, 0)
            lo_c = jnp.zeros(dw.shape, F32)
            lo_r = jnp.zeros(onehot_t.shape, F32)
            for gg in range(ng):
                lo = seg_ref[(tile * ng + gg) * 2].astype(F32)
                lo_c = jnp.where(lane == N_EXPERTS + gg, lo, lo_c)
                lo_r = jnp.where(sub == N_EXPERTS + gg, lo, lo_r)
            pos_c = jnp.sum(onehot * (rank_c + lo_c), axis=1, keepdims=True)
            pos_r = jnp.sum(onehot_t * (rank_r + lo_r), axis=0, keepdims=True)
            pos_s[k] = jnp.broadcast_to(pos_c, (tm, LANES))
            perm = jnp.where(pos_r.astype(jnp.int32) == row, 1.0, 0.0).astype(BF16)
            hs_s[k] = _dot(perm, h_ref[rows, :]).astype(BF16)
            d1 = dw.astype(BF16)
            r1 = dw - d1.astype(F32)
            d2 = r1.astype(BF16)
            d3 = (r1 - d2.astype(F32)).astype(BF16)
            dws_s[k] = _dot(perm, d1) + _dot(perm, d2) + _dot(perm, d3)
        acc_s[...] = jnp.zeros_like(acc_s)

    wd = wd_ref[...].reshape(MOE_EXPERTS_PER_STEP * D_EXPERT, wd_ref.shape[2])
    for k in range(MOE_TILES_PER_STEP):
        tile = pair * MOE_TILES_PER_STEP + k
        lo = seg_ref[(tile * ng + g) * 2]
        hi = seg_ref[(tile * ng + g) * 2 + 1]
        c_lo = lo // MOE_CHUNK
        c_hi = jnp.where(hi > lo, (hi + MOE_CHUNK - 1) // MOE_CHUNK, c_lo)

        def chunk(c, carry, k=k):
            rows = pl.ds(pl.multiple_of(c * MOE_CHUNK, MOE_CHUNK), MOE_CHUNK)
            hc = hs_s[k, rows, :]
            dwc = dws_s[k, rows, :]
            lane = lax.broadcasted_iota(jnp.int32, dwc.shape, 1)
            acts = []
            for e in range(MOE_EXPERTS_PER_STEP):
                gu = _dot(hc, wgu_ref[e])
                gate = gu[:, :D_EXPERT]
                wk = jnp.sum(jnp.where(lane == s * MOE_EXPERTS_PER_STEP + e, dwc, 0.0), axis=1, keepdims=True)
                acts.append((gate * _sigmoid(gate) * gu[:, D_EXPERT:] * wk).astype(BF16))
            y = _dot(jnp.concatenate(acts, axis=1), wd)
            acc_s[k, rows, :] = (acc_s[k, rows, :].astype(F32) + y).astype(BF16)
            return carry

        lax.fori_loop(c_lo, c_hi, chunk, 0)

    @pl.when(s == n_steps - 1)
    def _():
        col = lax.broadcasted_iota(jnp.int32, (tm, tm), 1)
        for k in range(MOE_TILES_PER_STEP):
            pos_c = pos_s[k][:, 0:1].astype(jnp.int32)
            unperm = jnp.where(pos_c == col, 1.0, 0.0).astype(BF16)
            o_ref[k * tm:(k + 1) * tm, :] = _dot(unperm, acc_s[k]).astype(BF16)


def _moe(h2, dw, wgu, wd):
    t, d = h2.shape
    tm = MOE_TILE
    n_tiles = t // tm
    tp = MOE_TILES_PER_STEP
    ne = MOE_EXPERTS_PER_STEP
    cnt = jnp.sum(dw[:, N_EXPERTS:N_EXPERTS + N_GROUPS].reshape(n_tiles, tm, N_GROUPS), axis=1).astype(jnp.int32)
    hi = jnp.cumsum(cnt, axis=1)
    seg = jnp.stack([hi - cnt, hi], axis=-1).reshape(-1)
    grid_spec = pltpu.PrefetchScalarGridSpec(
        num_scalar_prefetch=1,
        grid=(n_tiles // tp, N_EXPERTS // ne),
        in_specs=[pl.BlockSpec((tp * tm, d), lambda i, s, seg: (i, 0)),
                  pl.BlockSpec((tp * tm, LANES), lambda i, s, seg: (i, 0)),
                  pl.BlockSpec((ne, d, 2 * D_EXPERT), lambda i, s, seg: (s, 0, 0)),
                  pl.BlockSpec((ne, D_EXPERT, d), lambda i, s, seg: (s, 0, 0))],
        out_specs=pl.BlockSpec((tp * tm, d), lambda i, s, seg: (i, 0)),
        scratch_shapes=[pltpu.VMEM((tp, tm, d), BF16), pltpu.VMEM((tp, tm, LANES), F32),
                        pltpu.VMEM((tp, tm, d), BF16), pltpu.VMEM((tp, tm, LANES), F32)],
    )
    return pl.pallas_call(
        _moe_kernel,
        grid_spec=grid_spec,
        out_shape=jax.ShapeDtypeStruct((t, d), BF16),
        compiler_params=_cparams(("arbitrary", "arbitrary"), 56),
    )(seg, h2, dw, wgu, wd)


def _final_kernel(x_ref, y_ref, mod_ref, o_ref):
    d = D_MODEL
    o_ref[...] = x_ref[...] + mod_ref[:, 5 * d:6 * d] * y_ref[...].astype(F32)


def _final(xn, y, mods, nb, nt):
    d = D_MODEL
    blk = pl.BlockSpec((TILE, d), lambda b, j: (b * nt + 1 + j, 0))
    return pl.pallas_call(
        _final_kernel,
        grid=(nb, nt - 1),
        in_specs=[blk, blk, pl.BlockSpec((None, 1, 6 * d), lambda b, j: (b, 0, 0))],
        out_specs=pl.BlockSpec((None, TILE, d), lambda b, j: (b, j, 0)),
        out_shape=jax.ShapeDtypeStruct((nb, (nt - 1) * TILE, d), F32),
        compiler_params=_cparams(("arbitrary", "arbitrary"), 32),
    )(xn, y, mods)


def _rope_tables(n_tok, n_ctx):
    t = jnp.arange(n_tok)
    rowp = (t // GRID_W).astype(F32)
    colp = (t % GRID_W).astype(F32)
    n_freq = HEAD_DIM // 4
    inv = ROPE_THETA ** (-jnp.arange(n_freq, dtype=F32) / n_freq)
    ang = jnp.concatenate([rowp[:, None] * inv, colp[:, None] * inv], axis=-1)
    cos, sin = jnp.cos(ang), jnp.sin(ang)
    cos64 = jnp.concatenate([cos, cos], axis=-1)
    sin64 = jnp.concatenate([-sin, sin], axis=-1)
    pad = lambda a, v: jnp.concatenate([jnp.full((n_ctx, HEAD_DIM), v, F32), a], axis=0)
    return jnp.tile(pad(cos64, 1.0), (1, 4)), jnp.tile(pad(sin64, 0.0), (1, 4))


def kernel(x, c, ctx, c_ctx, w_mod, b_mod, g_norm, w_in, b_merge, g_qk, rpb, b_mlstm, g_ml, w_branch, w_out,
           w_group, b_group, w_router, b_router, w_gate_up, w_down):
    nb, n_tok, d = x.shape
    n_ctx = ctx.shape[1]
    depth = w_mod.shape[0]
    assert d == D_MODEL and n_ctx == TILE and n_tok % TILE == 0
    rows = n_tok // GRID_W
    nt = (n_ctx + n_tok) // TILE
    assert nt - 1 >= NA_WIN_TILES and rows >= NA_WIN_R
    t = nb * nt * TILE

    n_mod = -(-(nb + 1) // 8) * 8
    cc = jnp.zeros((n_mod, d), F32).at[:nb].set(c).at[nb].set(c_ctx)
    mods_all = _modulation(cc, w_mod, b_mod)

    cos_t, sin_t = _rope_tables(n_tok, n_ctx)
    gmat = jnp.asarray(np.kron(np.eye(256 // HEAD_DIM), np.ones((HEAD_DIM, HEAD_DIM))), BF16)
    a_cols = np.concatenate([np.arange(h * HEAD_DIM, (h + 1) * HEAD_DIM) for h in _A_HEAD_ORDER])
    na_tables = _na_table(rpb, rows)

    assert t % (MOE_TILE * MOE_TILES_PER_STEP) == 0
    stream = None
    for l in range(depth):
        mods = mods_all[l].reshape(n_mod, 1, 6 * d)
        wl = w_in[l]
        w_in_p = jnp.concatenate(
            [wl[:, a_cols], wl[:, 512:4352], jnp.pad(wl[:, 4352:4368], ((0, 0), (0, GATE_W - 16)))],
            axis=1).astype(BF16)
        assert w_in_p.shape[1] == IN_WIDTH_P
        wmg = wl[:, 4368:].astype(BF16)
        gqk = jnp.pad(jnp.tile(g_qk[l], (1, 256 // HEAD_DIM)), ((0, 4), (0, 0)))
        bcg = jnp.pad(b_mlstm[l].reshape(1, 16), ((0, 0), (0, GATE_W - 16)))
        bmg = b_merge[l].reshape(1, N_BRANCH * d)
        first = stream is None
        xs, qa, ka, vat, qb, kb, vbt, cq, ck, cv, co, cg = _in_projection(
            (x, ctx, mods) if first else stream, first, mods, g_norm[l, 0:1], w_in_p, gmat, gqk, cos_t, sin_t,
            bcg, nb, nt)

        ya = _attn_a(qa, ka, vat, nb, nt)
        yb = _attn_b(qb, kb, vbt, na_tables[l], nb, nt)
        hf, hb = _mlstm(cq, ck, cv, cg, nb, nt)

        wb = jnp.stack([w_branch[l, 0][a_cols], w_branch[l, 1], w_branch[l, 2]]).astype(BF16)
        w_rt = jnp.concatenate([w_router[l], w_group[l], jnp.zeros((d, LANES - N_EXPERTS - N_GROUPS), F32)], axis=1)
        wrh = w_rt.astype(BF16)
        wrl = (w_rt - wrh.astype(F32)).astype(BF16)
        wr3 = jnp.concatenate([wrh, wrl, wrh], axis=0)
        brt = jnp.concatenate([b_router[l], b_group[l], jnp.zeros((LANES - N_EXPERTS - N_GROUPS,), F32)])[None, :]
        xs, h2, dw = _merge(xs, ya, yb, hf, hb, co, mods, g_norm[l, 0:1], g_norm[l, 1:2], g_ml[l][None, :],
                            wmg, bmg, wb, w_out[l].astype(BF16), wr3, brt, nb, nt)
        stream = (xs, _moe(h2, dw, w_gate_up[l].astype(BF16), w_down[l].astype(BF16)), mods)

    return _final(*stream, nb, nt)
```
